```python
import math
import jax, jax.numpy as jnp
from jax import lax
import numpy as np

D_MODEL = 4096
BATCH = 4
SEQ = 2048
DEPTH = 1
DEC_BATCH = 128
DEC_SEQ = 1
PAST_LEN = 2048
PAGE_SIZE = 128

HEAD_DIM = 128
MIX_WIDTH = D_MODEL
A_WIDTH = MIX_WIDTH // 2
A_HEADS = A_WIDTH // HEAD_DIM
B_WIDTH = MIX_WIDTH - A_WIDTH
CHUNK = 128
B_GROUP_DIM = 128
B_GROUPS = B_WIDTH // B_GROUP_DIM
WINDOWS = (128, 512, 2048)
DILATIONS = (1, 4, 16)
MAX_WINDOW = max(WINDOWS)
BAND = 128
MEM_LEN = 256
MEM_HEADS = 4
MEM_HEAD_DIM = 128
MEM_WIDTH = MEM_HEADS * MEM_HEAD_DIM
FFN_HIDDEN = 4 * D_MODEL
ROPE_THETA = 10000.0
EPS = 1e-6
IN_WIDTH = 3 * A_WIDTH + 2 * B_WIDTH

kernel_name = 'hymba_dilated_gmlp_decoder_step'


def rms_norm(x, g):
    xf = x.astype(jnp.float32)
    y = xf * lax.rsqrt(jnp.mean(xf * xf, axis=-1, keepdims=True) + EPS)
    return (y * g.astype(jnp.float32)).astype(x.dtype)


def layer_norm(x, g, b):
    xf = x.astype(jnp.float32)
    xc = xf - jnp.mean(xf, axis=-1, keepdims=True)
    var = jnp.mean(xc * xc, axis=-1, keepdims=True)
    return (xc * lax.rsqrt(var + EPS) * g.astype(jnp.float32) + b.astype(jnp.float32)).astype(x.dtype)


def rope(x, pos):
    half = x.shape[-1] // 2
    inv = jnp.exp(-math.log(ROPE_THETA) * jnp.arange(half, dtype=jnp.float32) / half)
    ang = pos.astype(jnp.float32)[:, None] * inv[None, :]
    cos = jnp.cos(ang)[:, None, :]
    sin = jnp.sin(ang)[:, None, :]
    xf = x.astype(jnp.float32)
    x1, x2 = xf[..., :half], xf[..., half:]
    return jnp.concatenate([x1 * cos - x2 * sin, x2 * cos + x1 * sin], axis=-1).astype(x.dtype)


def mixer_inputs(x, pos, norm_g, w_in):
    n, t, _ = x.shape
    z = rms_norm(x, norm_g) @ w_in
    q = z[..., :A_WIDTH].reshape(n, t, A_HEADS, HEAD_DIM)
    k = z[..., A_WIDTH:2 * A_WIDTH].reshape(n, t, A_HEADS, HEAD_DIM)
    v = z[..., 2 * A_WIDTH:3 * A_WIDTH].reshape(n, t, A_HEADS, HEAD_DIM)
    u = z[..., 3 * A_WIDTH:3 * A_WIDTH + B_WIDTH]
    vb = z[..., 3 * A_WIDTH + B_WIDTH:]
    return rope(q, pos), rope(k, pos), v, u, vb


def dilated_prompt_one(q, k, v, window, dilation):
    n, t, h, e = q.shape
    steps = window // dilation
    L = t // dilation
    nb = -(-L // BAND)
    lp = nb * BAND

    def split(a):
        a = a.reshape(n, L, dilation, h, e).transpose(0, 2, 1, 3, 4)
        a = jnp.pad(a, ((0, 0), (0, 0), (0, lp - L), (0, 0), (0, 0)))
        return a.reshape(n, dilation, nb, BAND, h, e)

    def with_prev(a):
        prev = jnp.pad(a, ((0, 0), (0, 0), (1, 0), (0, 0), (0, 0), (0, 0)))[:, :, :-1]
        return jnp.concatenate([prev, a], axis=3)

    qb = split(q)
    kk = with_prev(split(k))
    vv = with_prev(split(v))
    s = jnp.einsum('ndbqhe,ndbkhe->ndbhqk', qb, kk, preferred_element_type=jnp.float32) * (e ** -0.5)
    dist = jnp.arange(BAND)[:, None] + BAND - jnp.arange(2 * BAND)[None, :]
    key_idx = jnp.arange(nb)[:, None] * BAND - BAND + jnp.arange(2 * BAND)[None, :]
    mask = ((dist >= 0) & (dist <= steps))[None] & (key_idx >= 0)[:, None, :]
    s = jnp.where(mask[None, None, :, None], s, -jnp.inf)
    m = jnp.max(s, axis=-1, keepdims=True)
    p = jnp.exp(s - m)
    den = jnp.sum(p, axis=-1, keepdims=True)
    o = jnp.einsum('ndbhqk,ndbkhe->ndbqhe', p / den, vv.astype(jnp.float32))
    lse = (m + jnp.log(den))[..., 0]
    o = o.reshape(n, dilation, lp, h, e)[:, :, :L].transpose(0, 2, 1, 3, 4).reshape(n, t, h, e)
    lse = lse.transpose(0, 1, 2, 4, 3).reshape(n, dilation, lp, h)[:, :, :L]
    lse = lse.transpose(0, 2, 1, 3).reshape(n, t, h)
    return o, lse


def dilated_sample_one(q, k_all, v_all, window, dilation):
    t = q.shape[1]
    e = q.shape[-1]
    wb = k_all.shape[1] - t
    steps = window // dilation
    idx = wb + jnp.arange(t)[:, None] - dilation * jnp.arange(steps + 1)[None, :]
    valid = idx >= 0
    idx = jnp.maximum(idx, 0)
    kg = k_all[:, idx]
    vg = v_all[:, idx]
    s = jnp.einsum('nthe,ntjhe->nthj', q, kg, preferred_element_type=jnp.float32) * (e ** -0.5)
    s = jnp.where(valid[None, :, None, :], s, -jnp.inf)
    m = jnp.max(s, axis=-1, keepdims=True)
    p = jnp.exp(s - m)
    den = jnp.sum(p, axis=-1, keepdims=True)
    o = jnp.einsum('nthj,ntjhe->nthe', p / den, vg.astype(jnp.float32))
    lse = (m + jnp.log(den))[..., 0]
    return o, lse


def combine_by_denominator(pairs, dtype):
    o = jnp.stack([pr[0] for pr in pairs], axis=0)
    w = jax.nn.softmax(jnp.stack([pr[1] for pr in pairs], axis=0), axis=0)
    return jnp.sum(w[..., None] * o, axis=0).astype(dtype)


def spatial_gate(u, vb, ln_g, ln_b, w_s, b_s):
    n, t, _ = vb.shape
    c = min(CHUNK, t)
    nc = t // c
    vn = layer_norm(vb, ln_g, ln_b)
    w = jnp.tril(w_s)[:, :c, :c]
    vr = vn.reshape(n, nc, c, B_GROUPS, B_GROUP_DIM)
    s = jnp.einsum('gqk,nckge->ncqge', w, vr, preferred_element_type=jnp.float32)
    s = s + b_s[:, :c].T.astype(jnp.float32)[None, None, :, :, None]
    out = (u.astype(jnp.float32) * s.reshape(n, t, B_WIDTH)).astype(u.dtype)
    return out, vn


def cross_attn(h, mk, mv, w_q, w_o):
    n, t, _ = h.shape
    q = (h @ w_q).reshape(n, t, MEM_HEADS, MEM_HEAD_DIM)
    s = jnp.einsum('nthe,nmhe->nhtm', q, mk, preferred_element_type=jnp.float32) * (MEM_HEAD_DIM ** -0.5)
    p = jax.nn.softmax(s, axis=-1)
    o = jnp.einsum('nhtm,nmhe->nthe', p, mv.astype(jnp.float32)).astype(h.dtype)
    return o.reshape(n, t, MEM_WIDTH) @ w_o


def sq_relu_ffn(h, w_up, w_down):
    a = jnp.maximum(h @ w_up, 0)
    return (a * a) @ w_down


def setup_inputs(seed: int = 0) -> dict:
    key = jax.random.key(seed)
    ks = jax.random.split(key, 24)
    f = jnp.float32
    wb = min(MAX_WINDOW, PAST_LEN)

    def nrm(k, shape, scale):
        return jax.random.normal(k, shape, f) * scale

    return {
        'x_prompt': nrm(ks[0], (BATCH, SEQ, D_MODEL), 1.0),
        'x_sample': nrm(ks[1], (DEC_BATCH, DEC_SEQ, D_MODEL), 1.0),
        'mem_prompt': nrm(ks[2], (BATCH, MEM_LEN, D_MODEL), 1.0),
        'cache_win_k': nrm(ks[3], (DEPTH, DEC_BATCH, wb, A_HEADS, HEAD_DIM), 1.0),
        'cache_win_v': nrm(ks[4], (DEPTH, DEC_BATCH, wb, A_HEADS, HEAD_DIM), 1.0),
        'cache_mem_k': nrm(ks[5], (DEPTH, DEC_BATCH, MEM_LEN, MEM_HEADS, MEM_HEAD_DIM), 1.0),
        'cache_mem_v': nrm(ks[6], (DEPTH, DEC_BATCH, MEM_LEN, MEM_HEADS, MEM_HEAD_DIM), 1.0),
        'norm_mix': 1.0 + nrm(ks[7], (DEPTH, D_MODEL), 0.01),
        'w_in': nrm(ks[8], (DEPTH, D_MODEL, IN_WIDTH), D_MODEL ** -0.5),
        'ln_v_g': 1.0 + nrm(ks[9], (DEPTH, B_WIDTH), 0.01),
        'ln_v_b': nrm(ks[10], (DEPTH, B_WIDTH), 0.01),
        'w_spatial': nrm(ks[11], (DEPTH, B_GROUPS, CHUNK, CHUNK), CHUNK ** -0.5),
        'b_spatial': 1.0 + nrm(ks[12], (DEPTH, B_GROUPS, CHUNK), 0.01),
        'w_out': nrm(ks[13], (DEPTH, MIX_WIDTH, D_MODEL), MIX_WIDTH ** -0.5),
        'norm_xattn': 1.0 + nrm(ks[14], (DEPTH, D_MODEL), 0.01),
        'w_xq': nrm(ks[15], (DEPTH, D_MODEL, MEM_WIDTH), D_MODEL ** -0.5),
        'w_xk': nrm(ks[16], (DEPTH, D_MODEL, MEM_WIDTH), D_MODEL ** -0.5),
        'w_xv': nrm(ks[17], (DEPTH, D_MODEL, MEM_WIDTH), D_MODEL ** -0.5),
        'w_xo': nrm(ks[18], (DEPTH, MEM_WIDTH, D_MODEL), MEM_WIDTH ** -0.5),
        'norm_ffn': 1.0 + nrm(ks[19], (DEPTH, D_MODEL), 0.01),
        'w_up': nrm(ks[20], (DEPTH, D_MODEL, FFN_HIDDEN), D_MODEL ** -0.5),
        'w_down': nrm(ks[21], (DEPTH, FFN_HIDDEN, D_MODEL), FFN_HIDDEN ** -0.5),
        'norm_final': 1.0 + nrm(ks[22], (D_MODEL,), 0.01),
    }


def reference(x_prompt, x_sample, mem_prompt, cache_win_k, cache_win_v, cache_mem_k, cache_mem_v,
              norm_mix, w_in, ln_v_g, ln_v_b, w_spatial, b_spatial, w_out,
              norm_xattn, w_xq, w_xk, w_xv, w_xo, norm_ffn, w_up, w_down, norm_final):
    n_p, t_p, _ = x_prompt.shape
    n_s, t_s, _ = x_sample.shape
    pos_p = jnp.arange(t_p, dtype=jnp.int32)
    pos_s = PAST_LEN + jnp.arange(t_s, dtype=jnp.int32)
    keep = min(MAX_WINDOW, t_p)
    xp, xs = x_prompt, x_sample
    pk, pv, pmk, pmv, sk, sv, sg = [], [], [], [], [], [], []
    for l in range(DEPTH):
        q, k, v, u, vb = mixer_inputs(xp, pos_p, norm_mix[l], w_in[l])
        a = combine_by_denominator([dilated_prompt_one(q, k, v, w, d) for w, d in zip(WINDOWS, DILATIONS)], xp.dtype)
        g, _ = spatial_gate(u, vb, ln_v_g[l], ln_v_b[l], w_spatial[l], b_spatial[l])
        xp = xp + jnp.concatenate([a.reshape(n_p, t_p, A_WIDTH), g], axis=-1) @ w_out[l]
        mk = (mem_prompt @ w_xk[l]).reshape(n_p, MEM_LEN, MEM_HEADS, MEM_HEAD_DIM)
        mv = (mem_prompt @ w_xv[l]).reshape(n_p, MEM_LEN, MEM_HEADS, MEM_HEAD_DIM)
        xp = xp + cross_attn(rms_norm(xp, norm_xattn[l]), mk, mv, w_xq[l], w_xo[l])
        xp = xp + sq_relu_ffn(rms_norm(xp, norm_ffn[l]), w_up[l], w_down[l])
        pk.append(k[:, t_p - keep:])
        pv.append(v[:, t_p - keep:])
        pmk.append(mk)
        pmv.append(mv)
        q, k, v, u, vb = mixer_inputs(xs, pos_s, norm_mix[l], w_in[l])
        k_all = jnp.concatenate([cache_win_k[l], k], axis=1)
        v_all = jnp.concatenate([cache_win_v[l], v], axis=1)
        a = combine_by_denominator([dilated_sample_one(q, k_all, v_all, w, d) for w, d in zip(WINDOWS, DILATIONS)], xs.dtype)
        g, vn = spatial_gate(u, vb, ln_v_g[l], ln_v_b[l], w_spatial[l], b_spatial[l])
        xs = xs + jnp.concatenate([a.reshape(n_s, t_s, A_WIDTH), g], axis=-1) @ w_out[l]
        xs = xs + cross_attn(rms_norm(xs, norm_xattn[l]), cache_mem_k[l], cache_mem_v[l], w_xq[l], w_xo[l])
        xs = xs + sq_relu_ffn(rms_norm(xs, norm_ffn[l]), w_up[l], w_down[l])
        sk.append(k)
        sv.append(v)
        sg.append(vn)
    y_prompt = rms_norm(xp, norm_final)
    y_sample = rms_norm(xs, norm_final)
    return (y_prompt, y_sample, jnp.stack(pk), jnp.stack(pv), jnp.stack(pmk), jnp.stack(pmv),
            jnp.stack(sk), jnp.stack(sv), jnp.stack(sg))
```

```python
import functools
import math

import jax
import jax.numpy as jnp
from jax import lax
from jax.experimental import pallas as pl
from jax.experimental.pallas import tpu as pltpu

D_MODEL = 4096
HEAD_DIM = 128
A_WIDTH = 2048
A_HEADS = 16
B_WIDTH = 2048
B_GROUPS = 16
CHUNK = 128
BAND = 128
STEPS = 128
DILATIONS = (1, 4, 16)
MEM_LEN = 256
MEM_HEADS = 4
MEM_WIDTH = 512
FFN_HIDDEN = 16384
ROPE_THETA = 10000.0
EPS = 1e-6
PAST_LEN = 2048

V7X_VMEM_BYTES = 64 * 1024 * 1024
VMEM_LIMIT = V7X_VMEM_BYTES - 8 * 1024 * 1024
LANES = 128

BF16 = jnp.bfloat16
F32 = jnp.float32


def _params(*sem):
    return pltpu.CompilerParams(dimension_semantics=sem, vmem_limit_bytes=VMEM_LIMIT)


def _rms(x, g):
    return x * lax.rsqrt(jnp.mean(x * x, axis=-1, keepdims=True) + EPS) * g


def _stack_norm_kernel(n_prompt_blocks, xp_ref, xs_ref, g_ref, xcat_ref, xn_ref):
    i = pl.program_id(0)

    @pl.when(i < n_prompt_blocks)
    def _():
        x = xp_ref[...]
        xcat_ref[...] = x
        xn_ref[...] = _rms(x, g_ref[...]).astype(BF16)

    @pl.when(i >= n_prompt_blocks)
    def _():
        x = xs_ref[...]
        xcat_ref[...] = x
        xn_ref[...] = _rms(x, g_ref[...]).astype(BF16)


def stack_and_norm(xp, xs, g):
    n_p, n_s = xp.shape[0], xs.shape[0]
    rows = n_s
    npb = n_p // rows
    r = n_p + n_s
    return pl.pallas_call(
        functools.partial(_stack_norm_kernel, npb),
        grid=(npb + 1,),
        in_specs=[pl.BlockSpec((rows, D_MODEL), lambda i: (jnp.minimum(i, npb - 1), 0)),
                  pl.BlockSpec((rows, D_MODEL), lambda i: (0, 0)),
                  pl.BlockSpec((1, D_MODEL), lambda i: (0, 0))],
        out_specs=[pl.BlockSpec((rows, D_MODEL), lambda i: (i, 0)),
                   pl.BlockSpec((rows, D_MODEL), lambda i: (i, 0))],
        out_shape=[jax.ShapeDtypeStruct((r, D_MODEL), F32),
                   jax.ShapeDtypeStruct((r, D_MODEL), BF16)],
        compiler_params=_params("arbitrary"),
        name="stack_norm",
    )(xp, xs, g)


def _norm_kernel(x_ref, g_ref, o_ref):
    o_ref[...] = _rms(x_ref[...], g_ref[...]).astype(o_ref.dtype)


def rms_norm_rows(x, g, rows):
    r = x.shape[0]
    return pl.pallas_call(
        _norm_kernel,
        grid=(r // rows,),
        in_specs=[pl.BlockSpec((rows, D_MODEL), lambda i: (i, 0)),
                  pl.BlockSpec((1, D_MODEL), lambda i: (0, 0))],
        out_specs=pl.BlockSpec((rows, D_MODEL), lambda i: (i, 0)),
        out_shape=jax.ShapeDtypeStruct((r, D_MODEL), BF16),
        compiler_params=_params("arbitrary"),
        name="rms_norm",
    )(x, g)


def _final_norm_kernel(n_prompt_blocks, x_ref, d_ref, g_ref, yp_ref, ys_ref):
    i = pl.program_id(0)
    y = _rms(x_ref[...] + d_ref[...], g_ref[...])

    @pl.when(i < n_prompt_blocks)
    def _():
        yp_ref[...] = y

    @pl.when(i >= n_prompt_blocks)
    def _():
        ys_ref[...] = y


def final_norm(x, d, g, n_p, n_s):
    rows = n_s
    npb = n_p // rows
    return pl.pallas_call(
        functools.partial(_final_norm_kernel, npb),
        grid=(npb + 1,),
        in_specs=[pl.BlockSpec((rows, D_MODEL), lambda i: (i, 0)),
                  pl.BlockSpec((rows, D_MODEL), lambda i: (i, 0)),
                  pl.BlockSpec((1, D_MODEL), lambda i: (0, 0))],
        out_specs=[pl.BlockSpec((rows, D_MODEL), lambda i: (jnp.minimum(i, npb - 1), 0)),
                   pl.BlockSpec((rows, D_MODEL), lambda i: (0, 0))],
        out_shape=[jax.ShapeDtypeStruct((n_p, D_MODEL), F32),
                   jax.ShapeDtypeStruct((n_s, D_MODEL), F32)],
        compiler_params=_params("arbitrary"),
        name="final_norm",
    )(x, d, g)


def _dot(a, b):
    return jnp.dot(a.astype(BF16), b.astype(BF16), preferred_element_type=F32)


def _rope_tile(z, cos, sin_signed):
    parts = []
    for h in range(z.shape[1] // HEAD_DIM):
        zh = z[:, h * HEAD_DIM:(h + 1) * HEAD_DIM]
        parts.append(zh * cos + pltpu.roll(zh, HEAD_DIM // 2, 1) * sin_signed)
    return jnp.concatenate(parts, axis=1)


def _in_proj_kernel(rope_tiles, x_ref, w_ref, cos_ref, sin_ref, z_ref):
    z = _dot(x_ref[...], w_ref[...])
    is_rope = pl.program_id(1) < rope_tiles

    @pl.when(is_rope)
    def _():
        z_ref[...] = _rope_tile(z, cos_ref[...], sin_ref[...])

    @pl.when(jnp.logical_not(is_rope))
    def _():
        z_ref[...] = z


def in_proj(xn, w, cos, sin_signed, tm, tn):
    r = xn.shape[0]
    n = w.shape[1]
    return pl.pallas_call(
        functools.partial(_in_proj_kernel, 2 * A_WIDTH // tn),
        grid=(r // tm, n // tn),
        in_specs=[pl.BlockSpec((tm, D_MODEL), lambda i, j: (i, 0)),
                  pl.BlockSpec((D_MODEL, tn), lambda i, j: (0, j)),
                  pl.BlockSpec((tm, HEAD_DIM), lambda i, j: (i, 0)),
                  pl.BlockSpec((tm, HEAD_DIM), lambda i, j: (i, 0))],
        out_specs=pl.BlockSpec((tm, tn), lambda i, j: (i, j)),
        out_shape=jax.ShapeDtypeStruct((r, n), F32),
        compiler_params=_params("arbitrary", "arbitrary"),
        name="in_proj",
    )(xn, w, cos, sin_signed)


def _mm_kernel(x_ref, w_ref, o_ref):
    o_ref[...] = _dot(x_ref[...], w_ref[...]).astype(o_ref.dtype)


def _mm_relu2_kernel(x_ref, w_ref, o_ref):
    a = jnp.maximum(_dot(x_ref[...], w_ref[...]), 0.0)
    o_ref[...] = (a * a).astype(o_ref.dtype)


def _mm_res_kernel(x_ref, w_ref, r_ref, o_ref):
    o_ref[...] = r_ref[...] + _dot(x_ref[...], w_ref[...])


def matmul(x, w, tm, tn, out_dtype, body=_mm_kernel, residual=None, name="matmul"):
    m, k = x.shape
    n = w.shape[1]
    in_specs = [pl.BlockSpec((tm, k), lambda i, j: (i, 0)),
                pl.BlockSpec((k, tn), lambda i, j: (0, j))]
    args = [x, w]
    if residual is not None:
        in_specs.append(pl.BlockSpec((tm, tn), lambda i, j: (i, j)))
        args.append(residual)
    return pl.pallas_call(
        body,
        grid=(m // tm, n // tn),
        in_specs=in_specs,
        out_specs=pl.BlockSpec((tm, tn), lambda i, j: (i, j)),
        out_shape=jax.ShapeDtypeStruct((m, n), out_dtype),
        compiler_params=_params("arbitrary", "arbitrary"),
        name=name,
    )(*args)


def _mm2_kernel(x_ref, wa_ref, wb_ref, oa_ref, ob_ref):
    x = x_ref[...].astype(BF16)
    oa_ref[...] = _dot(x, wa_ref[...])
    ob_ref[...] = _dot(x, wb_ref[...])


def matmul_pair(x, wa, wb, tm):
    m, k = x.shape
    n = wa.shape[1]
    return pl.pallas_call(
        _mm2_kernel,
        grid=(m // tm,),
        in_specs=[pl.BlockSpec((tm, k), lambda i: (i, 0)),
                  pl.BlockSpec((k, n), lambda i: (0, 0)),
                  pl.BlockSpec((k, n), lambda i: (0, 0))],
        out_specs=[pl.BlockSpec((tm, n), lambda i: (i, 0)),
                   pl.BlockSpec((tm, n), lambda i: (i, 0))],
        out_shape=[jax.ShapeDtypeStruct((m, n), F32), jax.ShapeDtypeStruct((m, n), F32)],
        compiler_params=_params("arbitrary"),
        name="mem_kv_proj",
    )(x, wa, wb)


def _out_proj_kernel(a_ref, g_ref, wa_ref, wg_ref, r_ref, o_ref):
    o_ref[...] = r_ref[...] + (_dot(a_ref[...], wa_ref[...]) + _dot(g_ref[...], wg_ref[...]))


def out_proj(a, g, w, res, tm, tn):
    r = a.shape[0]
    n = w.shape[1]
    return pl.pallas_call(
        _out_proj_kernel,
        grid=(r // tm, n // tn),
        in_specs=[pl.BlockSpec((tm, A_WIDTH), lambda i, j: (i, 0)),
                  pl.BlockSpec((tm, B_WIDTH), lambda i, j: (i, 0)),
                  pl.BlockSpec((A_WIDTH, tn), lambda i, j: (0, j)),
                  pl.BlockSpec((B_WIDTH, tn), lambda i, j: (1, j)),
                  pl.BlockSpec((tm, tn), lambda i, j: (i, j))],
        out_specs=pl.BlockSpec((tm, tn), lambda i, j: (i, j)),
        out_shape=jax.ShapeDtypeStruct((r, n), F32),
        compiler_params=_params("arbitrary", "arbitrary"),
        name="out_proj",
    )(a, g, w, w, res)


def _mm_acc_kernel(x_ref, w_ref, o_ref):
    @pl.when(pl.program_id(2) == 0)
    def _():
        o_ref[...] = jnp.zeros_like(o_ref)

    o_ref[...] += _dot(x_ref[...], w_ref[...])


def matmul_ksplit(x, w, tm, tn, tk):
    m, k = x.shape
    n = w.shape[1]
    return pl.pallas_call(
        _mm_acc_kernel,
        grid=(m // tm, n // tn, k // tk),
        in_specs=[pl.BlockSpec((tm, tk), lambda i, j, kk: (i, kk)),
                  pl.BlockSpec((tk, tn), lambda i, j, kk: (kk, j))],
        out_specs=pl.BlockSpec((tm, tn), lambda i, j, kk: (i, j)),
        out_shape=jax.ShapeDtypeStruct((m, n), F32),
        compiler_params=_params("arbitrary", "arbitrary", "arbitrary"),
        name="ffn_down",
    )(x, w)


def _dot_nt(a, b):
    return lax.dot_general(a.astype(BF16), b.astype(BF16), (((1,), (1,)), ((), ())),
                           preferred_element_type=F32)


def _prompt_attn_kernel(seq, q_ref, k_ref, v_ref, o_ref, m_ref, l_ref, acc_ref):
    scale = HEAD_DIM ** -0.5
    row = lax.broadcasted_iota(jnp.int32, (BAND, BAND), 0)
    col = lax.broadcasted_iota(jnp.int32, (BAND, BAND), 1)
    cur_mask = col <= row
    prev_mask = col >= row
    neg = -jnp.inf

    for d in DILATIONS:
        sub_len = seq // d
        for r in range(d):
            for b in range(sub_len // BAND):
                def rows(blk):
                    start = r + d * BAND * blk
                    return pl.ds(start, BAND) if d == 1 else pl.ds(start, BAND, stride=d)

                cur = rows(b)
                qb = q_ref[cur, :]
                vc = v_ref[cur, :]
                s_c = jnp.where(cur_mask, _dot_nt(qb, k_ref[cur, :]) * scale, neg)
                m_b = jnp.max(s_c, axis=-1, keepdims=True)
                if b > 0:
                    prev = rows(b - 1)
                    vp = v_ref[prev, :]
                    s_p = jnp.where(prev_mask, _dot_nt(qb, k_ref[prev, :]) * scale, neg)
                    m_b = jnp.maximum(m_b, jnp.max(s_p, axis=-1, keepdims=True))
                p_c = jnp.exp(s_c - m_b)
                l_b = jnp.sum(p_c, axis=-1, keepdims=True)
                acc_b = _dot(p_c, vc)
                if b > 0:
                    p_p = jnp.exp(s_p - m_b)
                    l_b = l_b + jnp.sum(p_p, axis=-1, keepdims=True)
                    acc_b = acc_b + _dot(p_p, vp)
                m_b = jnp.broadcast_to(m_b, (BAND, LANES))
                l_b = jnp.broadcast_to(l_b, (BAND, LANES))
                if d == DILATIONS[0]:
                    m_ref[cur, :] = m_b
                    l_ref[cur, :] = l_b
                    acc_ref[cur, :] = acc_b
                else:
                    m_old = m_ref[cur, :]
                    m_new = jnp.maximum(m_old, m_b)
                    w_old = jnp.exp(m_old - m_new)
                    w_b = jnp.exp(m_b - m_new)
                    m_ref[cur, :] = m_new
                    l_ref[cur, :] = w_old * l_ref[cur, :] + w_b * l_b
                    acc_ref[cur, :] = w_old * acc_ref[cur, :] + w_b * acc_b

    o_ref[...] = (acc_ref[...] / l_ref[...]).astype(o_ref.dtype)


def prompt_attention(z, n_batch, seq):
    def spec(seg):
        return pl.BlockSpec((seq, HEAD_DIM), lambda n, h: (n, seg * A_HEADS + h))

    q, k, v = z, z, z
    return pl.pallas_call(
        functools.partial(_prompt_attn_kernel, seq),
        grid=(n_batch, A_HEADS),
        in_specs=[spec(0), spec(1), spec(2)],
        out_specs=spec(0),
        out_shape=jax.ShapeDtypeStruct((n_batch * seq, A_WIDTH), BF16),
        scratch_shapes=[pltpu.VMEM((seq, LANES), F32)] * 3,
        compiler_params=_params("arbitrary", "arbitrary"),
        name="prompt_attn",
    )(q, k, v)


def _sample_attn_kernel(q_ref, kn_ref, vn_ref, k1_ref, k4_ref, k16_ref, v1_ref, v4_ref, v16_ref, o_ref):
    scale = HEAD_DIM ** -0.5
    q = q_ref[...] * scale
    s_new = jnp.sum(q * kn_ref[...], axis=-1, keepdims=True)
    scores = [jnp.sum(kr[...] * q[None], axis=-1, keepdims=True) for kr in (k1_ref, k4_ref, k16_ref)]
    m = s_new
    for s in scores:
        m = jnp.maximum(m, jnp.max(s, axis=0))
    p_new = float(len(DILATIONS)) * jnp.exp(s_new - m)
    den = p_new
    num = p_new * vn_ref[...]
    for s, vr in zip(scores, (v1_ref, v4_ref, v16_ref)):
        p = jnp.exp(s - m[None])
        den = den + jnp.sum(p, axis=0)
        num = num + jnp.sum(p * vr[...], axis=0)
    o_ref[...] = num / den


def sample_attention(q, k_new, v_new, cache_k, cache_v):
    n, wlen, n_heads, dim = cache_k.shape
    row_spec = pl.BlockSpec((None, n_heads, dim), lambda b: (b, 0, 0))
    views, specs = [], []
    for cache in (cache_k, cache_v):
        for d in DILATIONS:
            blocks = wlen // (d * STEPS)
            views.append(cache.reshape(n, blocks, STEPS, d, n_heads, dim))
            specs.append(pl.BlockSpec((None, None, STEPS, None, n_heads, dim),
                                      functools.partial(lambda blk, b: (b, blk, 0, 0, 0, 0), blocks - 1)))
    return pl.pallas_call(
        _sample_attn_kernel,
        grid=(n,),
        in_specs=[row_spec, row_spec, row_spec] + specs,
        out_specs=row_spec,
        out_shape=jax.ShapeDtypeStruct((n, n_heads, dim), F32),
        compiler_params=_params("arbitrary"),
        name="sample_attn",
    )(q, k_new, v_new, *views)


def _layer_norm(x, g, b):
    xc = x - jnp.mean(x, axis=-1, keepdims=True)
    var = jnp.mean(xc * xc, axis=-1, keepdims=True)
    return xc * lax.rsqrt(var + EPS) * g + b


def _prompt_gate_kernel(u_ref, vb_ref, g_ref, b_ref, w_ref, bs_ref, o_ref):
    vn = _layer_norm(vb_ref[...], g_ref[...], b_ref[...]).astype(BF16)
    row = lax.broadcasted_iota(jnp.int32, (CHUNK, CHUNK), 0)
    col = lax.broadcasted_iota(jnp.int32, (CHUNK, CHUNK), 1)
    causal = col <= row
    bs = bs_ref[...]
    for g in range(B_GROUPS):
        cols = slice(g * LANES, (g + 1) * LANES)
        w = jnp.where(causal, w_ref[g], 0.0)
        s = _dot(w, vn[:, cols]) + bs[:, g:g + 1]
        o_ref[:, cols] = (u_ref[:, cols] * s).astype(o_ref.dtype)


def prompt_gate(z, ln_g, ln_b, w_s, b_s_t, n_rows):
    spec = pl.BlockSpec((CHUNK, B_WIDTH), lambda c: (c, 0))
    vec = pl.BlockSpec((1, B_WIDTH), lambda c: (0, 0))
    u, vb = z, z
    return pl.pallas_call(
        _prompt_gate_kernel,
        grid=(n_rows // CHUNK,),
        in_specs=[pl.BlockSpec((CHUNK, B_WIDTH), lambda c: (c, 3)),
                  pl.BlockSpec((CHUNK, B_WIDTH), lambda c: (c, 4)), vec, vec,
                  pl.BlockSpec((B_GROUPS, CHUNK, CHUNK), lambda c: (0, 0, 0)),
                  pl.BlockSpec((CHUNK, B_GROUPS), lambda c: (0, 0))],
        out_specs=spec,
        out_shape=jax.ShapeDtypeStruct((n_rows, B_WIDTH), BF16),
        compiler_params=_params("arbitrary"),
        name="prompt_gate",
    )(u, vb, ln_g, ln_b, w_s, b_s_t)


def _sample_gate_kernel(u_ref, vb_ref, g_ref, b_ref, w0_ref, b0_ref, vn_ref, o_ref):
    vn = _layer_norm(vb_ref[...], g_ref[...], b_ref[...])
    vn_ref[...] = vn
    o_ref[...] = u_ref[...] * (w0_ref[...] * vn + b0_ref[...])


def sample_gate(u, vb, ln_g, ln_b, w0, b0):
    n = u.shape[0]
    full = pl.BlockSpec((n, B_WIDTH), lambda i: (0, 0))
    vec = pl.BlockSpec((1, B_WIDTH), lambda i: (0, 0))
    return pl.pallas_call(
        _sample_gate_kernel,
        grid=(1,),
        in_specs=[full, full, vec, vec, vec, vec],
        out_specs=[full, full],
        out_shape=[jax.ShapeDtypeStruct((n, B_WIDTH), F32), jax.ShapeDtypeStruct((n, B_WIDTH), F32)],
        compiler_params=_params("arbitrary"),
        name="sample_gate",
    )(u, vb, ln_g, ln_b, w0, b0)


def _prompt_xattn_kernel(q_ref, mk_ref, mv_ref, o_ref):
    scale = HEAD_DIM ** -0.5
    for h in range(MEM_HEADS):
        cols = slice(h * HEAD_DIM, (h + 1) * HEAD_DIM)
        s = _dot_nt(q_ref[:, cols], mk_ref[:, cols]) * scale
        p = jnp.exp(s - jnp.max(s, axis=-1, keepdims=True))
        den = jnp.sum(p, axis=-1, keepdims=True)
        o_ref[:, cols] = (_dot(p, mv_ref[:, cols]) / den).astype(o_ref.dtype)


def prompt_xattn(q, mk, mv, n_batch, seq, tq):
    per = seq // tq
    return pl.pallas_call(
        _prompt_xattn_kernel,
        grid=(n_batch, per),
        in_specs=[pl.BlockSpec((tq, MEM_WIDTH), lambda n, t: (n * per + t, 0)),
                  pl.BlockSpec((MEM_LEN, MEM_WIDTH), lambda n, t: (n, 0)),
                  pl.BlockSpec((MEM_LEN, MEM_WIDTH), lambda n, t: (n, 0))],
        out_specs=pl.BlockSpec((tq, MEM_WIDTH), lambda n, t: (n * per + t, 0)),
        out_shape=jax.ShapeDtypeStruct((n_batch * seq, MEM_WIDTH), BF16),
        compiler_params=_params("arbitrary", "arbitrary"),
        name="prompt_xattn",
    )(q, mk, mv)


def _sample_xattn_kernel(per_step, q_ref, mk_ref, mv_ref, o_ref):
    scale = HEAD_DIM ** -0.5
    for b in range(per_step):
        q = q_ref[b] * scale
        s = jnp.sum(mk_ref[b] * q[None], axis=-1, keepdims=True)
        p = jnp.exp(s - jnp.max(s, axis=0)[None])
        o_ref[b] = jnp.sum(p * mv_ref[b], axis=0) / jnp.sum(p, axis=0)


def sample_xattn(q, mk, mv, per_step):
    n, mem_len, n_heads, dim = mk.shape
    q_spec = pl.BlockSpec((per_step, n_heads, dim), lambda i: (i, 0, 0))
    mem_spec = pl.BlockSpec((per_step, mem_len, n_heads, dim), lambda i: (i, 0, 0, 0))
    return pl.pallas_call(
        functools.partial(_sample_xattn_kernel, per_step),
        grid=(n // per_step,),
        in_specs=[q_spec, mem_spec, mem_spec],
        out_specs=q_spec,
        out_shape=jax.ShapeDtypeStruct((n, n_heads, dim), F32),
        compiler_params=_params("arbitrary"),
        name="sample_xattn",
    )(q, mk, mv)


def _rope_tables(seq, n_batch, n_sample):
    half = HEAD_DIM // 2
    inv = jnp.exp(-math.log(ROPE_THETA) * jnp.arange(half, dtype=F32) / half)
    pos = jnp.concatenate([jnp.tile(jnp.arange(seq, dtype=jnp.int32), n_batch),
                           jnp.full((n_sample,), PAST_LEN, jnp.int32)])
    ang = pos.astype(F32)[:, None] * inv[None, :]
    cos, sin = jnp.cos(ang), jnp.sin(ang)
    return jnp.concatenate([cos, cos], axis=1), jnp.concatenate([-sin, sin], axis=1)


def kernel(x_prompt, x_sample, mem_prompt, cache_win_k, cache_win_v, cache_mem_k, cache_mem_v, norm_mix, w_in, ln_v_g, ln_v_b, w_spatial, b_spatial, w_out, norm_xattn, w_xq, w_xk, w_xv, w_xo, norm_ffn, w_up, w_down, norm_final):
    n_b, seq, _ = x_prompt.shape
    n_s = x_sample.shape[0]
    n_p = n_b * seq
    depth = w_in.shape[0]
    assert depth == 1 and x_sample.shape[1] == 1
    l = 0
    tm = 1040

    cos, sin_signed = _rope_tables(seq, n_b, n_s)
    row = lambda a: a.reshape(1, -1)

    x0, xn = stack_and_norm(x_prompt.reshape(n_p, D_MODEL), x_sample.reshape(n_s, D_MODEL), row(norm_mix[l]))
    z = in_proj(xn, w_in[l], cos, sin_signed, tm, 512)
    k = z[:, A_WIDTH:2 * A_WIDTH]
    v = z[:, 2 * A_WIDTH:3 * A_WIDTH]
    z_s = z[n_p:]
    q_s, k_s, v_s, u_s, vb_s = (z_s[:, c * A_WIDTH:(c + 1) * A_WIDTH] for c in range(5))

    assert seq == PAST_LEN and cache_win_k.shape[2] == PAST_LEN and seq == STEPS * DILATIONS[-1]
    a_p = prompt_attention(z, n_b, seq)
    by_head = lambda a: a.reshape(n_s, -1, HEAD_DIM)
    a_s = sample_attention(by_head(q_s), by_head(k_s), by_head(v_s), cache_win_k[l], cache_win_v[l])
    a = jnp.concatenate([a_p, a_s.reshape(n_s, A_WIDTH).astype(BF16)], axis=0)

    g_p = prompt_gate(z, row(ln_v_g[l]), row(ln_v_b[l]), w_spatial[l], b_spatial[l].T, n_p)
    w0 = jnp.repeat(w_spatial[l][:, 0, 0], CHUNK).reshape(1, B_WIDTH)
    b0 = jnp.repeat(b_spatial[l][:, 0], CHUNK).reshape(1, B_WIDTH)
    vn_s, g_s = sample_gate(u_s, vb_s, row(ln_v_g[l]), row(ln_v_b[l]), w0, b0)
    g = jnp.concatenate([g_p, g_s.astype(BF16)], axis=0)

    x1 = out_proj(a, g, w_out[l], x0, tm, 512)

    h1 = rms_norm_rows(x1, row(norm_xattn[l]), 640)
    qx = matmul(h1, w_xq[l], tm, MEM_WIDTH, F32, name="xattn_q")
    mk, mv = matmul_pair(mem_prompt.reshape(n_b * MEM_LEN, D_MODEL), w_xk[l], w_xv[l], 512)
    ox_p = prompt_xattn(qx, mk, mv, n_b, seq, 512)
    ox_s = sample_xattn(by_head(qx[n_p:]), cache_mem_k[l], cache_mem_v[l], 4)
    ox = jnp.concatenate([ox_p, ox_s.reshape(n_s, MEM_WIDTH).astype(BF16)], axis=0)
    x2 = matmul(ox, w_xo[l], tm, 512, F32, body=_mm_res_kernel, residual=x1, name="xattn_o")

    h2 = rms_norm_rows(x2, row(norm_ffn[l]), 640)
    act = matmul(h2, w_up[l], tm, 512, BF16, body=_mm_relu2_kernel, name="ffn_up")
    down = matmul_ksplit(act, w_down[l], 2080, 1024, 1024)
    y_p, y_s = final_norm(x2, down, row(norm_final), n_p, n_s)

    return (y_p.reshape(n_b, seq, D_MODEL),
            y_s.reshape(n_s, 1, D_MODEL),
            k[:n_p].reshape(depth, n_b, seq, A_HEADS, HEAD_DIM),
            v[:n_p].reshape(depth, n_b, seq, A_HEADS, HEAD_DIM),
            mk.reshape(depth, n_b, MEM_LEN, MEM_HEADS, HEAD_DIM),
            mv.reshape(depth, n_b, MEM_LEN, MEM_HEADS, HEAD_DIM),
            k[n_p:].reshape(depth, n_s, 1, A_HEADS, HEAD_DIM),
            v[n_p:].reshape(depth, n_s, 1, A_HEADS, HEAD_DIM),
            vn_s.reshape(depth, n_s, 1, B_WIDTH))
```

```python
import functools
import math

import jax
import jax.numpy as jnp
import numpy as np
from jax import lax
from jax.experimental import pallas as pl
from jax.experimental.pallas import tpu as pltpu

D_MODEL = 4096
HEAD_DIM = 128
A_WIDTH = 2048
A_HEADS = 16
B_WIDTH = 2048
B_GROUPS = 16
CHUNK = 128
BAND = 128
STEPS = 128
DILATIONS = (1, 4, 16)
MEM_LEN = 256
MEM_HEADS = 4
MEM_WIDTH = 512
FFN_HIDDEN = 16384
ROPE_THETA = 10000.0
EPS = 1e-6
PAST_LEN = 2048

V7X_VMEM_BYTES = 64 * 1024 * 1024
VMEM_LIMIT = V7X_VMEM_BYTES - 4 * 1024 * 1024
LANES = 128

BF16 = jnp.bfloat16
F32 = jnp.float32


def _params(*sem):
    return pltpu.CompilerParams(dimension_semantics=sem, vmem_limit_bytes=VMEM_LIMIT)


def _rms(x, g):
    return x * lax.rsqrt(jnp.mean(x * x, axis=-1, keepdims=True) + EPS) * g


def _stack_norm_kernel(n_prompt_blocks, n_s, xp_ref, xs_ref, g_ref, xcat_ref, xn_ref):
    i = pl.program_id(0)

    @pl.when(i < n_prompt_blocks)
    def _():
        x = xp_ref[...]
        xcat_ref[...] = x
        xn_ref[...] = _rms(x, g_ref[...]).astype(BF16)

    @pl.when(i >= n_prompt_blocks)
    def _():
        x = xs_ref[...]
        xcat_ref[:n_s, :] = x
        xn_ref[:n_s, :] = _rms(x, g_ref[...]).astype(BF16)


def stack_and_norm(xp, xs, g, rows):
    n_p, n_s = xp.shape[0], xs.shape[0]
    assert n_p % rows == 0 and n_s <= rows
    npb = n_p // rows
    r = n_p + n_s
    return pl.pallas_call(
        functools.partial(_stack_norm_kernel, npb, n_s),
        grid=(npb + 1,),
        in_specs=[pl.BlockSpec((rows, D_MODEL), lambda i: (jnp.minimum(i, npb - 1), 0)),
                  pl.BlockSpec((n_s, D_MODEL), lambda i: (0, 0)),
                  pl.BlockSpec((1, D_MODEL), lambda i: (0, 0))],
        out_specs=[pl.BlockSpec((rows, D_MODEL), lambda i: (i, 0)),
                   pl.BlockSpec((rows, D_MODEL), lambda i: (i, 0))],
        out_shape=[jax.ShapeDtypeStruct((r, D_MODEL), F32),
                   jax.ShapeDtypeStruct((r, D_MODEL), BF16)],
        compiler_params=_params("arbitrary"),
        name="stack_norm",
    )(xp, xs, g)


def _norm_kernel(x_ref, g_ref, o_ref):
    o_ref[...] = _rms(x_ref[...], g_ref[...]).astype(o_ref.dtype)


def rms_norm_rows(x, g, rows):
    r = x.shape[0]
    return pl.pallas_call(
        _norm_kernel,
        grid=(r // rows,),
        in_specs=[pl.BlockSpec((rows, D_MODEL), lambda i: (i, 0)),
                  pl.BlockSpec((1, D_MODEL), lambda i: (0, 0))],
        out_specs=pl.BlockSpec((rows, D_MODEL), lambda i: (i, 0)),
        out_shape=jax.ShapeDtypeStruct((r, D_MODEL), BF16),
        compiler_params=_params("arbitrary"),
        name="rms_norm",
    )(x, g)


def _final_norm_kernel(n_prompt_blocks, n_s, x_ref, g_ref, yp_ref, ys_ref):
    i = pl.program_id(0)

    @pl.when(i < n_prompt_blocks)
    def _():
        yp_ref[...] = _rms(x_ref[...], g_ref[...])

    @pl.when(i >= n_prompt_blocks)
    def _():
        ys_ref[...] = _rms(x_ref[:n_s, :], g_ref[...])


def final_norm(x, g, n_p, n_s, rows):
    assert n_p % rows == 0 and n_s <= rows
    npb = n_p // rows
    return pl.pallas_call(
        functools.partial(_final_norm_kernel, npb, n_s),
        grid=(npb + 1,),
        in_specs=[pl.BlockSpec((rows, D_MODEL), lambda i: (i, 0)),
                  pl.BlockSpec((1, D_MODEL), lambda i: (0, 0))],
        out_specs=[pl.BlockSpec((rows, D_MODEL), lambda i: (jnp.minimum(i, npb - 1), 0)),
                   pl.BlockSpec((n_s, D_MODEL), lambda i: (0, 0))],
        out_shape=[jax.ShapeDtypeStruct((n_p, D_MODEL), F32),
                   jax.ShapeDtypeStruct((n_s, D_MODEL), F32)],
        compiler_params=_params("arbitrary"),
        name="final_norm",
    )(x, g)


def _dot(a, b):
    return jnp.dot(a.astype(BF16), b.astype(BF16), preferred_element_type=F32)


def _rope_tile(z, cos, sin_signed):
    parts = []
    for h in range(z.shape[1] // HEAD_DIM):
        zh = z[:, h * HEAD_DIM:(h + 1) * HEAD_DIM]
        parts.append(zh * cos + pltpu.roll(zh, HEAD_DIM // 2, 1) * sin_signed)
    return jnp.concatenate(parts, axis=1)


def _in_proj_kernel(x_ref, w_ref, cos_ref, sin_ref, z_ref):
    z_ref[...] = _rope_tile(_dot(x_ref[...], w_ref[...]), cos_ref[...], sin_ref[...])


def in_proj(xn, w, cos, sin_signed, tm, tn):
    r = xn.shape[0]
    n = w.shape[1]
    rope_tiles = 2 * A_WIDTH // tn
    table_map = lambda i, j: (jnp.where(j < rope_tiles, i, r // tm), 0)
    return pl.pallas_call(
        _in_proj_kernel,
        grid=(r // tm, n // tn),
        in_specs=[pl.BlockSpec((tm, D_MODEL), lambda i, j: (i, 0)),
                  pl.BlockSpec((D_MODEL, tn), lambda i, j: (0, j)),
                  pl.BlockSpec((tm, HEAD_DIM), table_map),
                  pl.BlockSpec((tm, HEAD_DIM), table_map)],
        out_specs=pl.BlockSpec((tm, tn), lambda i, j: (i, j)),
        out_shape=jax.ShapeDtypeStruct((r, n), F32),
        compiler_params=_params("arbitrary", "arbitrary"),
        name="in_proj",
    )(xn, w, cos, sin_signed)


def _mm_kernel(x_ref, w_ref, o_ref):
    o_ref[...] = _dot(x_ref[...], w_ref[...]).astype(o_ref.dtype)


def _mm_relu2_kernel(x_ref, w_ref, o_ref):
    a = jnp.maximum(_dot(x_ref[...], w_ref[...]), 0.0)
    o_ref[...] = (a * a).astype(o_ref.dtype)


def _mm_res_kernel(x_ref, w_ref, r_ref, o_ref):
    o_ref[...] = r_ref[...] + _dot(x_ref[...], w_ref[...])


def matmul(x, w, tm, tn, out_dtype, body=_mm_kernel, residual=None, name="matmul", x_buffers=2):
    m, k = x.shape
    n = w.shape[1]
    in_specs = [pl.BlockSpec((tm, k), lambda i, j: (i, 0), pipeline_mode=pl.Buffered(x_buffers)),
                pl.BlockSpec((k, tn), lambda i, j: (0, j))]
    args = [x, w]
    if residual is not None:
        in_specs.append(pl.BlockSpec((tm, tn), lambda i, j: (i, j)))
        args.append(residual)
    return pl.pallas_call(
        body,
        grid=(m // tm, n // tn),
        in_specs=in_specs,
        out_specs=pl.BlockSpec((tm, tn), lambda i, j: (i, j)),
        out_shape=jax.ShapeDtypeStruct((m, n), out_dtype),
        compiler_params=_params("arbitrary", "arbitrary"),
        name=name,
    )(*args)


def _mm2_kernel(x_ref, wa_ref, wb_ref, oa_ref, ob_ref):
    x = x_ref[...].astype(BF16)
    oa_ref[...] = _dot(x, wa_ref[...])
    ob_ref[...] = _dot(x, wb_ref[...])


def matmul_pair(x, wa, wb, tm):
    m, k = x.shape
    n = wa.shape[1]
    return pl.pallas_call(
        _mm2_kernel,
        grid=(m // tm,),
        in_specs=[pl.BlockSpec((tm, k), lambda i: (i, 0)),
                  pl.BlockSpec((k, n), lambda i: (0, 0)),
                  pl.BlockSpec((k, n), lambda i: (0, 0))],
        out_specs=[pl.BlockSpec((tm, n), lambda i: (i, 0)),
                   pl.BlockSpec((tm, n), lambda i: (i, 0))],
        out_shape=[jax.ShapeDtypeStruct((m, n), F32), jax.ShapeDtypeStruct((m, n), F32)],
        compiler_params=_params("arbitrary"),
        name="mem_kv_proj",
    )(x, wa, wb)


def _out_proj_kernel(a_ref, g_ref, wa_ref, wg_ref, r_ref, o_ref):
    o_ref[...] = r_ref[...] + (_dot(a_ref[...], wa_ref[...]) + _dot(g_ref[...], wg_ref[...]))


def out_proj(a, g, w, res, tm, tn):
    r = a.shape[0]
    n = w.shape[1]
    return pl.pallas_call(
        _out_proj_kernel,
        grid=(r // tm, n // tn),
        in_specs=[pl.BlockSpec((tm, A_WIDTH), lambda i, j: (i, 0)),
                  pl.BlockSpec((tm, B_WIDTH), lambda i, j: (i, 0)),
                  pl.BlockSpec((A_WIDTH, tn), lambda i, j: (0, j)),
                  pl.BlockSpec((B_WIDTH, tn), lambda i, j: (1, j)),
                  pl.BlockSpec((tm, tn), lambda i, j: (i, j))],
        out_specs=pl.BlockSpec((tm, tn), lambda i, j: (i, j)),
        out_shape=jax.ShapeDtypeStruct((r, n), F32),
        compiler_params=_params("arbitrary", "arbitrary"),
        name="out_proj",
    )(a, g, w, w, res)


def _mm_acc_kernel(row_chunk, x_ref, w_ref, r_ref, o_ref):
    @pl.when(pl.program_id(2) == 0)
    def _():
        o_ref[...] = r_ref[...]

    w = w_ref[...].astype(BF16)
    for start in range(0, x_ref.shape[0], row_chunk):
        rows = slice(start, start + row_chunk)
        o_ref[rows, :] += _dot(x_ref[rows, :], w)


def matmul_ksplit_residual(x, w, res, tm, tn, tk, row_chunk):
    m, k = x.shape
    n = w.shape[1]
    assert tm % row_chunk == 0
    return pl.pallas_call(
        functools.partial(_mm_acc_kernel, row_chunk),
        grid=(m // tm, n // tn, k // tk),
        in_specs=[pl.BlockSpec((tm, tk), lambda i, j, kk: (i, kk)),
                  pl.BlockSpec((tk, tn), lambda i, j, kk: (kk, j)),
                  pl.BlockSpec((tm, tn), lambda i, j, kk: (i, j))],
        out_specs=pl.BlockSpec((tm, tn), lambda i, j, kk: (i, j)),
        out_shape=jax.ShapeDtypeStruct((m, n), F32),
        compiler_params=_params("arbitrary", "arbitrary", "arbitrary"),
        name="ffn_down",
    )(x, w, res)


def _dot_nt(a, b):
    return lax.dot_general(a.astype(BF16), b.astype(BF16), (((1,), (1,)), ((), ())),
                           preferred_element_type=F32)


def _window_bias(n_blocks, blocks_per_seq):
    row = np.arange(BAND)[:, None]
    col = np.arange(2 * BAND)[None, :]
    cur = (col >= BAND) & (col - BAND <= row)
    prev = (col < BAND) & (col >= row)
    has_prev = (np.arange(n_blocks) % blocks_per_seq != 0)[:, None, None]
    ok = cur[None] | (prev[None] & has_prev)
    return np.where(ok, 0.0, -np.inf).astype(np.float32)


def _softmax_pv(s, v_aug):
    m = jnp.max(s, axis=-1, keepdims=True)
    p = jnp.exp(s - m).astype(BF16)
    r = jnp.einsum('bqk,bkn->bqn', p, v_aug, preferred_element_type=F32)
    acc = r[..., :HEAD_DIM]
    return jnp.broadcast_to(m, acc.shape), r[..., HEAD_DIM:], acc


def _merge_softmax(a, b):
    m = jnp.maximum(a[0], b[0])
    wa = jnp.exp(a[0] - m)
    wb = jnp.exp(b[0] - m)
    return m, wa * a[1] + wb * b[1], wa * a[2] + wb * b[2]


def _prompt_attn_kernel(seq, q_ref, k_ref, v_ref, bias_ref, o_ref,
                        q4_ref, k4_ref, v4_ref, qb_ref, kw_ref, vw_ref, m_ref, l_ref, acc_ref):
    nblk = seq // BAND
    quarter = seq // 4
    scale = HEAD_DIM ** -0.5
    blocks = lambda x: x.reshape(nblk, BAND, x.shape[-1])
    flat = lambda x: x.reshape(seq, x.shape[-1])

    def qk(keys):
        return jnp.einsum('bqe,bke->bqk', qb_ref[...], keys, preferred_element_type=F32) * scale

    def windowed(q, k, v, bias):
        kb = blocks(k.astype(BF16))
        vb = with_ones(blocks(v.astype(BF16)))
        qb_ref[...] = blocks(q.astype(BF16))
        kw_ref[:, BAND:, :] = kb
        kw_ref[1:, :BAND, :] = kb[:-1]
        vw_ref[:, BAND:, :] = vb
        vw_ref[1:, :BAND, :] = vb[:-1]
        return _softmax_pv(qk(kw_ref[...]) + bias, vw_ref[...])

    def with_ones(v):
        return jnp.concatenate([v, jnp.ones(v.shape, BF16)], axis=-1)

    for r in range(4):
        dst = slice(r * quarter, (r + 1) * quarter)
        src = pl.ds(r, quarter, stride=4)
        q4_ref[dst, :] = q_ref[src, :]
        k4_ref[dst, :] = k_ref[src, :]
        v4_ref[dst, :] = v_ref[src, :]

    kw_ref[0, :BAND, :] = jnp.zeros((BAND, HEAD_DIM), BF16)
    vw_ref[0, :BAND, :] = jnp.zeros((BAND, 2 * HEAD_DIM), BF16)

    def rows16(j):
        return pl.ds((j % 4) * quarter + j // 4, BAND, stride=4)

    for j in range(nblk):
        qb_ref[j] = q4_ref[rows16(j), :].astype(BF16)
        kw_ref[j, BAND:, :] = k4_ref[rows16(j), :].astype(BF16)
        vw_ref[j, BAND:, :] = with_ones(v4_ref[rows16(j), :].astype(BF16))
    m, l, acc = _softmax_pv(qk(kw_ref[:, BAND:, :]) + bias_ref[0, 0, :, BAND:], vw_ref[:, BAND:, :])
    for j in range(nblk):
        m_ref[rows16(j), :] = m[j]
        l_ref[rows16(j), :] = l[j]
        acc_ref[rows16(j), :] = acc[j]

    part = windowed(q4_ref[...], k4_ref[...], v4_ref[...], bias_ref[1])
    m, l, acc = _merge_softmax((blocks(m_ref[...]), blocks(l_ref[...]), blocks(acc_ref[...])), part)
    m, l, acc = flat(m), flat(l), flat(acc)
    for r in range(4):
        src = slice(r * quarter, (r + 1) * quarter)
        dst = pl.ds(r, quarter, stride=4)
        m_ref[dst, :] = m[src]
        l_ref[dst, :] = l[src]
        acc_ref[dst, :] = acc[src]

    part = windowed(q_ref[...], k_ref[...], v_ref[...], bias_ref[0])
    _, l, acc = _merge_softmax((blocks(m_ref[...]), blocks(l_ref[...]), blocks(acc_ref[...])), part)
    o_ref[...] = flat(acc / l).astype(o_ref.dtype)


def prompt_attention(z, n_batch, seq):
    def spec(seg):
        return pl.BlockSpec((seq, HEAD_DIM), lambda n, h: (n, seg * A_HEADS + h))

    nblk = seq // BAND
    bias = jnp.asarray(np.stack([_window_bias(nblk, nblk), _window_bias(nblk, nblk // 4)]))
    rows = pltpu.VMEM((seq, LANES), F32)
    return pl.pallas_call(
        functools.partial(_prompt_attn_kernel, seq),
        grid=(n_batch, A_HEADS),
        in_specs=[spec(0), spec(1), spec(2), pl.BlockSpec(bias.shape, lambda n, h: (0, 0, 0, 0))],
        out_specs=spec(0),
        out_shape=jax.ShapeDtypeStruct((n_batch * seq, A_WIDTH), BF16),
        scratch_shapes=[rows, rows, rows,
                        pltpu.VMEM((nblk, BAND, HEAD_DIM), BF16),
                        pltpu.VMEM((nblk, 2 * BAND, HEAD_DIM), BF16),
                        pltpu.VMEM((nblk, 2 * BAND, 2 * HEAD_DIM), BF16),
                        rows, rows, rows],
        compiler_params=_params("arbitrary", "arbitrary"),
        name="prompt_attn",
    )(z, z, z, bias)


def _sample_attn_kernel(q_ref, kn_ref, vn_ref, k1_ref, k4_ref, k16_ref, v1_ref, v4_ref, v16_ref, o_ref):
    scale = HEAD_DIM ** -0.5
    q = q_ref[...] * scale
    s_new = jnp.sum(q * kn_ref[...], axis=-1, keepdims=True)
    scores = [jnp.sum(kr[...] * q[None], axis=-1, keepdims=True) for kr in (k1_ref, k4_ref, k16_ref)]
    m = s_new
    for s in scores:
        m = jnp.maximum(m, jnp.max(s, axis=0))
    p_new = float(len(DILATIONS)) * jnp.exp(s_new - m)
    den = p_new
    num = p_new * vn_ref[...]
    for s, vr in zip(scores, (v1_ref, v4_ref, v16_ref)):
        p = jnp.exp(s - m[None])
        den = den + jnp.sum(p, axis=0)
        num = num + jnp.sum(p * vr[...], axis=0)
    o_ref[...] = num / den


def sample_attention(q, k_new, v_new, cache_k, cache_v):
    n, wlen, n_heads, dim = cache_k.shape
    row_spec = pl.BlockSpec((None, n_heads, dim), lambda b: (b, 0, 0))
    views, specs = [], []
    for cache in (cache_k, cache_v):
        for d in DILATIONS:
            blocks = wlen // (d * STEPS)
            views.append(cache.reshape(n, blocks, STEPS, d, n_heads, dim))
            specs.append(pl.BlockSpec((None, None, STEPS, None, n_heads, dim),
                                      functools.partial(lambda blk, b: (b, blk, 0, 0, 0, 0), blocks - 1)))
    return pl.pallas_call(
        _sample_attn_kernel,
        grid=(n,),
        in_specs=[row_spec, row_spec, row_spec] + specs,
        out_specs=row_spec,
        out_shape=jax.ShapeDtypeStruct((n, n_heads, dim), F32),
        compiler_params=_params("arbitrary"),
        name="sample_attn",
    )(q, k_new, v_new, *views)


def _layer_norm(x, g, b):
    xc = x - jnp.mean(x, axis=-1, keepdims=True)
    var = jnp.mean(xc * xc, axis=-1, keepdims=True)
    return xc * lax.rsqrt(var + EPS) * g + b


def _prompt_gate_kernel(u_ref, vb_ref, g_ref, b_ref, w_ref, bs_ref, o_ref):
    vn = _layer_norm(vb_ref[...], g_ref[...], b_ref[...]).astype(BF16)
    row = lax.broadcasted_iota(jnp.int32, (CHUNK, CHUNK), 0)
    col = lax.broadcasted_iota(jnp.int32, (CHUNK, CHUNK), 1)
    causal = col <= row
    bs = bs_ref[...]
    for g in range(B_GROUPS):
        cols = slice(g * LANES, (g + 1) * LANES)
        w = jnp.where(causal, w_ref[g], 0.0)
        s = _dot(w, vn[:, cols]) + bs[:, g:g + 1]
        o_ref[:, cols] = (u_ref[:, cols] * s).astype(o_ref.dtype)


def prompt_gate(z, ln_g, ln_b, w_s, b_s_t, n_rows):
    spec = pl.BlockSpec((CHUNK, B_WIDTH), lambda c: (c, 0))
    vec = pl.BlockSpec((1, B_WIDTH), lambda c: (0, 0))
    u, vb = z, z
    return pl.pallas_call(
        _prompt_gate_kernel,
        grid=(n_rows // CHUNK,),
        in_specs=[pl.BlockSpec((CHUNK, B_WIDTH), lambda c: (c, 3)),
                  pl.BlockSpec((CHUNK, B_WIDTH), lambda c: (c, 4)), vec, vec,
                  pl.BlockSpec((B_GROUPS, CHUNK, CHUNK), lambda c: (0, 0, 0)),
                  pl.BlockSpec((CHUNK, B_GROUPS), lambda c: (0, 0))],
        out_specs=spec,
        out_shape=jax.ShapeDtypeStruct((n_rows, B_WIDTH), BF16),
        compiler_params=_params("arbitrary"),
        name="prompt_gate",
    )(u, vb, ln_g, ln_b, w_s, b_s_t)


def _sample_gate_kernel(u_ref, vb_ref, g_ref, b_ref, w0_ref, b0_ref, vn_ref, o_ref):
    vn = _layer_norm(vb_ref[...], g_ref[...], b_ref[...])
    vn_ref[...] = vn
    o_ref[...] = u_ref[...] * (w0_ref[...] * vn + b0_ref[...])


def sample_gate(u, vb, ln_g, ln_b, w0, b0):
    n = u.shape[0]
    full = pl.BlockSpec((n, B_WIDTH), lambda i: (0, 0))
    vec = pl.BlockSpec((1, B_WIDTH), lambda i: (0, 0))
    return pl.pallas_call(
        _sample_gate_kernel,
        grid=(1,),
        in_specs=[full, full, vec, vec, vec, vec],
        out_specs=[full, full],
        out_shape=[jax.ShapeDtypeStruct((n, B_WIDTH), F32), jax.ShapeDtypeStruct((n, B_WIDTH), F32)],
        compiler_params=_params("arbitrary"),
        name="sample_gate",
    )(u, vb, ln_g, ln_b, w0, b0)


def _prompt_xattn_kernel(q_ref, mk_ref, mv_ref, o_ref):
    scale = HEAD_DIM ** -0.5
    for h in range(MEM_HEADS):
        cols = slice(h * HEAD_DIM, (h + 1) * HEAD_DIM)
        s = _dot_nt(q_ref[:, cols], mk_ref[:, cols]) * scale
        p = jnp.exp(s - jnp.max(s, axis=-1, keepdims=True))
        den = jnp.sum(p, axis=-1, keepdims=True)
        o_ref[:, cols] = (_dot(p, mv_ref[:, cols]) / den).astype(o_ref.dtype)


def prompt_xattn(q, mk, mv, n_batch, seq, tq):
    per = seq // tq
    return pl.pallas_call(
        _prompt_xattn_kernel,
        grid=(n_batch, per),
        in_specs=[pl.BlockSpec((tq, MEM_WIDTH), lambda n, t: (n * per + t, 0)),
                  pl.BlockSpec((MEM_LEN, MEM_WIDTH), lambda n, t: (n, 0)),
                  pl.BlockSpec((MEM_LEN, MEM_WIDTH), lambda n, t: (n, 0))],
        out_specs=pl.BlockSpec((tq, MEM_WIDTH), lambda n, t: (n * per + t, 0)),
        out_shape=jax.ShapeDtypeStruct((n_batch * seq, MEM_WIDTH), BF16),
        compiler_params=_params("arbitrary", "arbitrary"),
        name="prompt_xattn",
    )(q, mk, mv)


def _sample_xattn_kernel(per_step, q_ref, mk_ref, mv_ref, o_ref):
    scale = HEAD_DIM ** -0.5
    for b in range(per_step):
        q = q_ref[b] * scale
        s = jnp.sum(mk_ref[b] * q[None], axis=-1, keepdims=True)
        p = jnp.exp(s - jnp.max(s, axis=0)[None])
        o_ref[b] = jnp.sum(p * mv_ref[b], axis=0) / jnp.sum(p, axis=0)


def sample_xattn(q, mk, mv, per_step):
    n, mem_len, n_heads, dim = mk.shape
    q_spec = pl.BlockSpec((per_step, n_heads, dim), lambda i: (i, 0, 0))
    mem_spec = pl.BlockSpec((per_step, mem_len, n_heads, dim), lambda i: (i, 0, 0, 0))
    return pl.pallas_call(
        functools.partial(_sample_xattn_kernel, per_step),
        grid=(n // per_step,),
        in_specs=[q_spec, mem_spec, mem_spec],
        out_specs=q_spec,
        out_shape=jax.ShapeDtypeStruct((n, n_heads, dim), F32),
        compiler_params=_params("arbitrary"),
        name="sample_xattn",
    )(q, mk, mv)


def _rope_tables(seq, n_batch, n_sample, identity_rows):
    half = HEAD_DIM // 2
    inv = jnp.exp(-math.log(ROPE_THETA) * jnp.arange(half, dtype=F32) / half)
    pos = jnp.concatenate([jnp.tile(jnp.arange(seq, dtype=jnp.int32), n_batch),
                           jnp.full((n_sample,), PAST_LEN, jnp.int32)])
    ang = pos.astype(F32)[:, None] * inv[None, :]
    cos, sin = jnp.cos(ang), jnp.sin(ang)
    cos_full = jnp.concatenate([cos, cos], axis=1)
    sin_signed = jnp.concatenate([-sin, sin], axis=1)
    return (jnp.concatenate([cos_full, jnp.ones((identity_rows, HEAD_DIM), F32)], axis=0),
            jnp.concatenate([sin_signed, jnp.zeros((identity_rows, HEAD_DIM), F32)], axis=0))


def kernel(x_prompt, x_sample, mem_prompt, cache_win_k, cache_win_v, cache_mem_k, cache_mem_v, norm_mix, w_in, ln_v_g, ln_v_b, w_spatial, b_spatial, w_out, norm_xattn, w_xq, w_xk, w_xv, w_xo, norm_ffn, w_up, w_down, norm_final):
    n_b, seq, _ = x_prompt.shape
    n_s = x_sample.shape[0]
    n_p = n_b * seq
    depth = w_in.shape[0]
    assert depth == 1 and x_sample.shape[1] == 1
    l = 0
    tm = 1040

    cos, sin_signed = _rope_tables(seq, n_b, n_s, tm)
    row = lambda a: a.reshape(1, -1)

    x0, xn = stack_and_norm(x_prompt.reshape(n_p, D_MODEL), x_sample.reshape(n_s, D_MODEL), row(norm_mix[l]), 512)
    z = in_proj(xn, w_in[l], cos, sin_signed, tm, 512)
    k = z[:, A_WIDTH:2 * A_WIDTH]
    v = z[:, 2 * A_WIDTH:3 * A_WIDTH]
    z_s = z[n_p:]
    q_s, k_s, v_s, u_s, vb_s = (z_s[:, c * A_WIDTH:(c + 1) * A_WIDTH] for c in range(5))

    assert seq == PAST_LEN and cache_win_k.shape[2] == PAST_LEN and seq == STEPS * DILATIONS[-1]
    a_p = prompt_attention(z, n_b, seq)
    by_head = lambda a: a.reshape(n_s, -1, HEAD_DIM)
    a_s = sample_attention(by_head(q_s), by_head(k_s), by_head(v_s), cache_win_k[l], cache_win_v[l])
    a = jnp.concatenate([a_p, a_s.reshape(n_s, A_WIDTH).astype(BF16)], axis=0)

    g_p = prompt_gate(z, row(ln_v_g[l]), row(ln_v_b[l]), w_spatial[l], b_spatial[l].T, n_p)
    w0 = jnp.repeat(w_spatial[l][:, 0, 0], CHUNK).reshape(1, B_WIDTH)
    b0 = jnp.repeat(b_spatial[l][:, 0], CHUNK).reshape(1, B_WIDTH)
    vn_s, g_s = sample_gate(u_s, vb_s, row(ln_v_g[l]), row(ln_v_b[l]), w0, b0)
    g = jnp.concatenate([g_p, g_s.astype(BF16)], axis=0)

    x1 = out_proj(a, g, w_out[l], x0, tm, 512)

    h1 = rms_norm_rows(x1, row(norm_xattn[l]), 640)
    qx = matmul(h1, w_xq[l], tm, MEM_WIDTH, F32, name="xattn_q")
    mk, mv = matmul_pair(mem_prompt.reshape(n_b * MEM_LEN, D_MODEL), w_xk[l], w_xv[l], 512)
    ox_p = prompt_xattn(qx, mk, mv, n_b, seq, 512)
    ox_s = sample_xattn(by_head(qx[n_p:]), cache_mem_k[l], cache_mem_v[l], 4)
    ox = jnp.concatenate([ox_p, ox_s.reshape(n_s, MEM_WIDTH).astype(BF16)], axis=0)
    x2 = matmul(ox, w_xo[l], tm, 512, F32, body=_mm_res_kernel, residual=x1, name="xattn_o")

    h2 = rms_norm_rows(x2, row(norm_ffn[l]), 640)
    act = matmul(h2, w_up[l], tm, 1024, BF16, body=_mm_relu2_kernel, name="ffn_up", x_buffers=1)
    x3 = matmul_ksplit_residual(act, w_down[l], x2, 2 * tm, 1024, 1024, tm)
    y_p, y_s = final_norm(x3, row(norm_final), n_p, n_s, 512)

    return (y_p.reshape(n_b, seq, D_MODEL),
            y_s.reshape(n_s, 1, D_MODEL),
            k[:n_p].reshape(depth, n_b, seq, A_HEADS, HEAD_DIM),
            v[:n_p].reshape(depth, n_b, seq, A_HEADS, HEAD_DIM),
            mk.reshape(depth, n_b, MEM_LEN, MEM_HEADS, HEAD_DIM),
            mv.reshape(depth, n_b, MEM_LEN, MEM_HEADS, HEAD_DIM),
            k[n_p:].reshape(depth, n_s, 1, A_HEADS, HEAD_DIM),
            v[n_p:].reshape(depth, n_s, 1, A_HEADS, HEAD_DIM),
            vn_s.reshape(depth, n_s, 1, B_WIDTH))
```

```python
import functools
import math

import jax
import jax.numpy as jnp
import numpy as np
from jax import lax
from jax.experimental import pallas as pl
from jax.experimental.pallas import tpu as pltpu

D_MODEL = 4096
HEAD_DIM = 128
A_WIDTH = 2048
A_HEADS = 16
B_WIDTH = 2048
B_GROUPS = 16
CHUNK = 128
BAND = 128
STEPS = 128
DILATIONS = (1, 4, 16)
MEM_LEN = 256
MEM_HEADS = 4
MEM_WIDTH = 512
FFN_HIDDEN = 16384
ROPE_THETA = 10000.0
EPS = 1e-6
PAST_LEN = 2048

V7X_VMEM_BYTES = 64 * 1024 * 1024
VMEM_LIMIT = V7X_VMEM_BYTES - 4 * 1024 * 1024
LANES = 128

BF16 = jnp.bfloat16
F32 = jnp.float32


def _params(*sem):
    return pltpu.CompilerParams(dimension_semantics=sem, vmem_limit_bytes=VMEM_LIMIT)


def _rms(x, g):
    return x * lax.rsqrt(jnp.mean(x * x, axis=-1, keepdims=True) + EPS) * g


def _stack_norm_kernel(n_prompt_blocks, n_s, xp_ref, xs_ref, g_ref, xn_ref):
    i = pl.program_id(0)

    @pl.when(i < n_prompt_blocks)
    def _():
        xn_ref[...] = _rms(xp_ref[...], g_ref[...]).astype(BF16)

    @pl.when(i >= n_prompt_blocks)
    def _():
        xn_ref[:n_s, :] = _rms(xs_ref[...], g_ref[...]).astype(BF16)


def stack_and_norm(xp, xs, g, rows):
    n_p, n_s = xp.shape[0], xs.shape[0]
    assert n_p % rows == 0 and n_s <= rows
    npb = n_p // rows
    return pl.pallas_call(
        functools.partial(_stack_norm_kernel, npb, n_s),
        grid=(npb + 1,),
        in_specs=[pl.BlockSpec((rows, D_MODEL), lambda i: (jnp.minimum(i, npb - 1), 0)),
                  pl.BlockSpec((n_s, D_MODEL), lambda i: (0, 0)),
                  pl.BlockSpec((1, D_MODEL), lambda i: (0, 0))],
        out_specs=pl.BlockSpec((rows, D_MODEL), lambda i: (i, 0)),
        out_shape=jax.ShapeDtypeStruct((n_p + n_s, D_MODEL), BF16),
        compiler_params=_params("arbitrary"),
        name="stack_norm",
    )(xp, xs, g)


def _final_norm_kernel(n_prompt_blocks, n_s, x_ref, g_ref, yp_ref, ys_ref):
    i = pl.program_id(0)

    @pl.when(i < n_prompt_blocks)
    def _():
        yp_ref[...] = _rms(x_ref[...], g_ref[...])

    @pl.when(i >= n_prompt_blocks)
    def _():
        ys_ref[...] = _rms(x_ref[:n_s, :], g_ref[...])


def final_norm(x, g, n_p, n_s, rows):
    assert n_p % rows == 0 and n_s <= rows
    npb = n_p // rows
    return pl.pallas_call(
        functools.partial(_final_norm_kernel, npb, n_s),
        grid=(npb + 1,),
        in_specs=[pl.BlockSpec((rows, D_MODEL), lambda i: (i, 0)),
                  pl.BlockSpec((1, D_MODEL), lambda i: (0, 0))],
        out_specs=[pl.BlockSpec((rows, D_MODEL), lambda i: (jnp.minimum(i, npb - 1), 0)),
                   pl.BlockSpec((n_s, D_MODEL), lambda i: (0, 0))],
        out_shape=[jax.ShapeDtypeStruct((n_p, D_MODEL), F32),
                   jax.ShapeDtypeStruct((n_s, D_MODEL), F32)],
        compiler_params=_params("arbitrary"),
        name="final_norm",
    )(x, g)


def _dot(a, b):
    return jnp.dot(a.astype(BF16), b.astype(BF16), preferred_element_type=F32)


def _in_proj_kernel(x_ref, w_ref, cos_ref, sin_ref, z_ref):
    z = _dot(x_ref[...], w_ref[...])
    cos = cos_ref[...]
    sin_signed = sin_ref[...]
    for h in range(z_ref.shape[0]):
        zh = z[:, h * HEAD_DIM:(h + 1) * HEAD_DIM]
        z_ref[h] = zh * cos + pltpu.roll(zh, HEAD_DIM // 2, 1) * sin_signed


def in_proj(xn, w, cos, sin_signed, tm, tn):
    r = xn.shape[0]
    n = w.shape[1]
    rope_tiles = 2 * A_WIDTH // tn
    slabs = tn // HEAD_DIM
    table_map = lambda i, j: (jnp.where(j < rope_tiles, i, r // tm), 0)
    return pl.pallas_call(
        _in_proj_kernel,
        grid=(r // tm, n // tn),
        in_specs=[pl.BlockSpec((tm, D_MODEL), lambda i, j: (i, 0)),
                  pl.BlockSpec((D_MODEL, tn), lambda i, j: (0, j)),
                  pl.BlockSpec((tm, HEAD_DIM), table_map),
                  pl.BlockSpec((tm, HEAD_DIM), table_map)],
        out_specs=pl.BlockSpec((slabs, tm, HEAD_DIM), lambda i, j: (j, i, 0)),
        out_shape=jax.ShapeDtypeStruct((n // HEAD_DIM, r, HEAD_DIM), F32),
        compiler_params=_params("arbitrary", "arbitrary"),
        name="in_proj",
    )(xn, w, cos, sin_signed)


def _mm_kernel(x_ref, w_ref, o_ref):
    o_ref[...] = _dot(x_ref[...], w_ref[...]).astype(o_ref.dtype)


def _mm_relu2_kernel(x_ref, w_ref, o_ref):
    a = jnp.maximum(_dot(x_ref[...], w_ref[...]), 0.0)
    o_ref[...] = (a * a).astype(o_ref.dtype)


def matmul(x, w, tm, tn, out_dtype, body=_mm_kernel, residual=None, name="matmul", x_buffers=2):
    m, k = x.shape
    n = w.shape[1]
    in_specs = [pl.BlockSpec((tm, k), lambda i, j: (i, 0), pipeline_mode=pl.Buffered(x_buffers)),
                pl.BlockSpec((k, tn), lambda i, j: (0, j))]
    args = [x, w]
    if residual is not None:
        in_specs.append(pl.BlockSpec((tm, tn), lambda i, j: (i, j)))
        args.append(residual)
    return pl.pallas_call(
        body,
        grid=(m // tm, n // tn),
        in_specs=in_specs,
        out_specs=pl.BlockSpec((tm, tn), lambda i, j: (i, j)),
        out_shape=jax.ShapeDtypeStruct((m, n), out_dtype),
        compiler_params=_params("arbitrary", "arbitrary"),
        name=name,
    )(*args)


def _mm2_kernel(x_ref, wa_ref, wb_ref, oa_ref, ob_ref):
    x = x_ref[...].astype(BF16)
    oa_ref[...] = _dot(x, wa_ref[...])
    ob_ref[...] = _dot(x, wb_ref[...])


def matmul_pair(x, wa, wb, tm):
    m, k = x.shape
    n = wa.shape[1]
    return pl.pallas_call(
        _mm2_kernel,
        grid=(m // tm,),
        in_specs=[pl.BlockSpec((tm, k), lambda i: (i, 0)),
                  pl.BlockSpec((k, n), lambda i: (0, 0)),
                  pl.BlockSpec((k, n), lambda i: (0, 0))],
        out_specs=[pl.BlockSpec((tm, n), lambda i: (i, 0)),
                   pl.BlockSpec((tm, n), lambda i: (i, 0))],
        out_shape=[jax.ShapeDtypeStruct((m, n), F32), jax.ShapeDtypeStruct((m, n), F32)],
        compiler_params=_params("arbitrary"),
        name="mem_kv_proj",
    )(x, wa, wb)


def _out_proj_kernel(head, a_ref, g_ref, wa_ref, wg_ref, rp_ref, rs_ref, o_ref):
    mix = _dot(a_ref[...], wa_ref[...]) + _dot(g_ref[...], wg_ref[...])
    is_last = pl.program_id(0) == pl.num_programs(0) - 1

    @pl.when(jnp.logical_not(is_last))
    def _():
        o_ref[...] = rp_ref[...] + mix

    @pl.when(is_last)
    def _():
        o_ref[:head, :] = rp_ref[:head, :] + mix[:head]
        o_ref[head:, :] = rs_ref[...] + mix[head:]


def out_proj(a, g, w, res_p, res_s, tm, tn):
    r = a.shape[0]
    n = w.shape[1]
    n_p, n_s = res_p.shape[0], res_s.shape[0]
    head = tm - n_s
    assert r == n_p + n_s and r % tm == 0 and n_p == (r // tm - 1) * tm + head and head % 8 == 0
    return pl.pallas_call(
        functools.partial(_out_proj_kernel, head),
        grid=(r // tm, n // tn),
        in_specs=[pl.BlockSpec((tm, A_WIDTH), lambda i, j: (i, 0)),
                  pl.BlockSpec((tm, B_WIDTH), lambda i, j: (i, 0)),
                  pl.BlockSpec((A_WIDTH, tn), lambda i, j: (0, j)),
                  pl.BlockSpec((B_WIDTH, tn), lambda i, j: (1, j)),
                  pl.BlockSpec((tm, tn), lambda i, j: (i, j)),
                  pl.BlockSpec((n_s, tn), lambda i, j: (0, j))],
        out_specs=pl.BlockSpec((tm, tn), lambda i, j: (i, j)),
        out_shape=jax.ShapeDtypeStruct((r, n), F32),
        compiler_params=_params("arbitrary", "arbitrary"),
        name="out_proj",
    )(a, g, w, w, res_p, res_s)


def _norm_proj_kernel(x_ref, g_ref, w_ref, o_ref):
    o_ref[...] = _dot(_rms(x_ref[...], g_ref[...]), w_ref[...])


def norm_proj(x, g, w, tm):
    r = x.shape[0]
    n = w.shape[1]
    return pl.pallas_call(
        _norm_proj_kernel,
        grid=(r // tm,),
        in_specs=[pl.BlockSpec((tm, D_MODEL), lambda i: (i, 0)),
                  pl.BlockSpec((1, D_MODEL), lambda i: (0, 0)),
                  pl.BlockSpec((D_MODEL, n), lambda i: (0, 0))],
        out_specs=pl.BlockSpec((tm, n), lambda i: (i, 0)),
        out_shape=jax.ShapeDtypeStruct((r, n), F32),
        compiler_params=_params("arbitrary"),
        name="xattn_q",
    )(x, g, w)


def _proj_res_norm_kernel(o_in_ref, w_ref, r_ref, g_ref, x_ref, h_ref):
    x = r_ref[...] + _dot(o_in_ref[...], w_ref[...])
    x_ref[...] = x
    h_ref[...] = _rms(x, g_ref[...]).astype(BF16)


def proj_res_norm(o_in, w, res, g, tm):
    r, k = o_in.shape
    return pl.pallas_call(
        _proj_res_norm_kernel,
        grid=(r // tm,),
        in_specs=[pl.BlockSpec((tm, k), lambda i: (i, 0)),
                  pl.BlockSpec((k, D_MODEL), lambda i: (0, 0)),
                  pl.BlockSpec((tm, D_MODEL), lambda i: (i, 0)),
                  pl.BlockSpec((1, D_MODEL), lambda i: (0, 0))],
        out_specs=[pl.BlockSpec((tm, D_MODEL), lambda i: (i, 0)),
                   pl.BlockSpec((tm, D_MODEL), lambda i: (i, 0))],
        out_shape=[jax.ShapeDtypeStruct((r, D_MODEL), F32),
                   jax.ShapeDtypeStruct((r, D_MODEL), BF16)],
        compiler_params=_params("arbitrary"),
        name="xattn_o",
    )(o_in, w, res, g)


def _write_tail_kernel(buf_ref, tail_ref, o_ref):
    del buf_ref
    o_ref[...] = tail_ref[...].astype(o_ref.dtype)


def write_tail(buf, tail):
    n_t, width = tail.shape
    assert buf.shape[0] % n_t == 0 and buf.shape[1] == width
    return pl.pallas_call(
        _write_tail_kernel,
        grid=(1,),
        in_specs=[pl.BlockSpec(memory_space=pl.ANY),
                  pl.BlockSpec((n_t, width), lambda i: (0, 0))],
        out_specs=pl.BlockSpec((n_t, width), lambda i: (buf.shape[0] // n_t - 1, 0)),
        out_shape=jax.ShapeDtypeStruct(buf.shape, buf.dtype),
        input_output_aliases={0: 0},
        compiler_params=_params("arbitrary"),
        name="write_tail",
    )(buf, tail)


def _mm_acc_kernel(row_chunk, x_ref, w_ref, r_ref, o_ref):
    @pl.when(pl.program_id(2) == 0)
    def _():
        o_ref[...] = r_ref[...]

    w = w_ref[...].astype(BF16)
    for start in range(0, x_ref.shape[0], row_chunk):
        rows = slice(start, start + row_chunk)
        o_ref[rows, :] += _dot(x_ref[rows, :], w)


def matmul_ksplit_residual(x, w, res, tm, tn, tk, row_chunk):
    m, k = x.shape
    n = w.shape[1]
    assert tm % row_chunk == 0
    return pl.pallas_call(
        functools.partial(_mm_acc_kernel, row_chunk),
        grid=(m // tm, n // tn, k // tk),
        in_specs=[pl.BlockSpec((tm, tk), lambda i, j, kk: (i, kk)),
                  pl.BlockSpec((tk, tn), lambda i, j, kk: (kk, j)),
                  pl.BlockSpec((tm, tn), lambda i, j, kk: (i, j))],
        out_specs=pl.BlockSpec((tm, tn), lambda i, j, kk: (i, j)),
        out_shape=jax.ShapeDtypeStruct((m, n), F32),
        compiler_params=_params("arbitrary", "arbitrary", "arbitrary"),
        name="ffn_down",
    )(x, w, res)


def _dot_nt(a, b):
    return lax.dot_general(a.astype(BF16), b.astype(BF16), (((1,), (1,)), ((), ())),
                           preferred_element_type=F32)


def _window_bias(n_blocks, blocks_per_seq):
    row = np.arange(BAND)[:, None]
    col = np.arange(2 * BAND)[None, :]
    cur = (col >= BAND) & (col - BAND <= row)
    prev = (col < BAND) & (col >= row)
    has_prev = (np.arange(n_blocks) % blocks_per_seq != 0)[:, None, None]
    ok = cur[None] | (prev[None] & has_prev)
    return np.where(ok, 0.0, -np.inf).astype(np.float32)


def _softmax_pv(s, v_aug):
    m = jnp.max(s, axis=-1, keepdims=True)
    p = jnp.exp(s - m).astype(BF16)
    r = jnp.einsum('bqk,bkn->bqn', p, v_aug, preferred_element_type=F32)
    acc = r[..., :HEAD_DIM]
    return jnp.broadcast_to(m, acc.shape), r[..., HEAD_DIM:], acc


def _merge_softmax(a, b):
    m = jnp.maximum(a[0], b[0])
    wa = jnp.exp(a[0] - m)
    wb = jnp.exp(b[0] - m)
    return m, wa * a[1] + wb * b[1], wa * a[2] + wb * b[2]


def _prompt_attn_kernel(seq, heads, q_ref, k_ref, v_ref, bias_ref, o_ref, *scratch):
    for hh in range(heads):
        out = _attend_head(seq, q_ref.at[hh], k_ref.at[hh], v_ref.at[hh], bias_ref, *scratch)
        o_ref[:, hh * HEAD_DIM:(hh + 1) * HEAD_DIM] = out.astype(o_ref.dtype)


def _attend_head(seq, q_ref, k_ref, v_ref, bias_ref,
                 q4_ref, k4_ref, v4_ref, qb_ref, kw_ref, vw_ref, m_ref, l_ref, acc_ref):
    nblk = seq // BAND
    quarter = seq // 4
    scale = HEAD_DIM ** -0.5
    blocks = lambda x: x.reshape(nblk, BAND, x.shape[-1])
    flat = lambda x: x.reshape(seq, x.shape[-1])

    def qk(keys):
        return jnp.einsum('bqe,bke->bqk', qb_ref[...], keys, preferred_element_type=F32) * scale

    def windowed(q, k, v, bias):
        kb = blocks(k.astype(BF16))
        vb = with_ones(blocks(v.astype(BF16)))
        qb_ref[...] = blocks(q.astype(BF16))
        kw_ref[:, BAND:, :] = kb
        kw_ref[1:, :BAND, :] = kb[:-1]
        vw_ref[:, BAND:, :] = vb
        vw_ref[1:, :BAND, :] = vb[:-1]
        return _softmax_pv(qk(kw_ref[...]) + bias, vw_ref[...])

    def with_ones(v):
        return jnp.concatenate([v, jnp.ones(v.shape, BF16)], axis=-1)

    for r in range(4):
        dst = slice(r * quarter, (r + 1) * quarter)
        src = pl.ds(r, quarter, stride=4)
        q4_ref[dst, :] = q_ref[src, :]
        k4_ref[dst, :] = k_ref[src, :]
        v4_ref[dst, :] = v_ref[src, :]

    kw_ref[0, :BAND, :] = jnp.zeros((BAND, HEAD_DIM), BF16)
    vw_ref[0, :BAND, :] = jnp.zeros((BAND, 2 * HEAD_DIM), BF16)

    def rows16(j):
        return pl.ds((j % 4) * quarter + j // 4, BAND, stride=4)

    for j in range(nblk):
        qb_ref[j] = q4_ref[rows16(j), :].astype(BF16)
        kw_ref[j, BAND:, :] = k4_ref[rows16(j), :].astype(BF16)
        vw_ref[j, BAND:, :] = with_ones(v4_ref[rows16(j), :].astype(BF16))
    m, l, acc = _softmax_pv(qk(kw_ref[:, BAND:, :]) + bias_ref[0, 0, :, BAND:], vw_ref[:, BAND:, :])
    for j in range(nblk):
        m_ref[rows16(j), :] = m[j]
        l_ref[rows16(j), :] = l[j]
        acc_ref[rows16(j), :] = acc[j]

    part = windowed(q4_ref[...], k4_ref[...], v4_ref[...], bias_ref[1])
    m, l, acc = _merge_softmax((blocks(m_ref[...]), blocks(l_ref[...]), blocks(acc_ref[...])), part)
    m, l, acc = flat(m), flat(l), flat(acc)
    for r in range(4):
        src = slice(r * quarter, (r + 1) * quarter)
        dst = pl.ds(r, quarter, stride=4)
        m_ref[dst, :] = m[src]
        l_ref[dst, :] = l[src]
        acc_ref[dst, :] = acc[src]

    part = windowed(q_ref[...], k_ref[...], v_ref[...], bias_ref[0])
    _, l, acc = _merge_softmax((blocks(m_ref[...]), blocks(l_ref[...]), blocks(acc_ref[...])), part)
    return flat(acc / l)


def prompt_attention(z, n_batch, seq, heads):
    assert A_HEADS % heads == 0
    groups = A_HEADS // heads

    def spec(seg):
        return pl.BlockSpec((heads, seq, HEAD_DIM), lambda n, h: (seg * groups + h, n, 0))

    nblk = seq // BAND
    bias = jnp.asarray(np.stack([_window_bias(nblk, nblk), _window_bias(nblk, nblk // 4)]))
    rows = pltpu.VMEM((seq, LANES), F32)
    return pl.pallas_call(
        functools.partial(_prompt_attn_kernel, seq, heads),
        grid=(n_batch, groups),
        in_specs=[spec(0), spec(1), spec(2), pl.BlockSpec(bias.shape, lambda n, h: (0, 0, 0, 0))],
        out_specs=pl.BlockSpec((seq, heads * HEAD_DIM), lambda n, h: (n, h)),
        out_shape=jax.ShapeDtypeStruct((z.shape[1], A_WIDTH), BF16),
        scratch_shapes=[rows, rows, rows,
                        pltpu.VMEM((nblk, BAND, HEAD_DIM), BF16),
                        pltpu.VMEM((nblk, 2 * BAND, HEAD_DIM), BF16),
                        pltpu.VMEM((nblk, 2 * BAND, 2 * HEAD_DIM), BF16),
                        rows, rows, rows],
        compiler_params=_params("arbitrary", "arbitrary"),
        name="prompt_attn",
    )(z, z, z, bias)


def _sample_attn_kernel(per_step, q_ref, kn_ref, vn_ref, k1_ref, k4_ref, k16_ref, v1_ref, v4_ref, v16_ref, o_ref):
    scale = HEAD_DIM ** -0.5
    for b in range(per_step):
        q = q_ref[b] * scale
        s_new = jnp.sum(q * kn_ref[b], axis=-1, keepdims=True)
        scores = [jnp.sum(kr[b] * q[None], axis=-1, keepdims=True) for kr in (k1_ref, k4_ref, k16_ref)]
        m = s_new
        for s in scores:
            m = jnp.maximum(m, jnp.max(s, axis=0))
        p_new = float(len(DILATIONS)) * jnp.exp(s_new - m)
        den = p_new
        num = p_new * vn_ref[b]
        for s, vr in zip(scores, (v1_ref, v4_ref, v16_ref)):
            p = jnp.exp(s - m[None])
            den = den + jnp.sum(p, axis=0)
            num = num + jnp.sum(p * vr[b], axis=0)
        o_ref[b] = num / den


def sample_attention(q, k_new, v_new, cache_k, cache_v, per_step):
    n, wlen, n_heads, dim = cache_k.shape
    row_spec = pl.BlockSpec((per_step, n_heads, dim), lambda b: (b, 0, 0))
    views, specs = [], []
    for cache in (cache_k, cache_v):
        for d in DILATIONS:
            blocks = wlen // (d * STEPS)
            views.append(cache.reshape(n, blocks, STEPS, d, n_heads, dim))
            specs.append(pl.BlockSpec((per_step, None, STEPS, None, n_heads, dim),
                                      functools.partial(lambda blk, b: (b, blk, 0, 0, 0, 0), blocks - 1)))
    return pl.pallas_call(
        functools.partial(_sample_attn_kernel, per_step),
        grid=(n // per_step,),
        in_specs=[row_spec, row_spec, row_spec] + specs,
        out_specs=row_spec,
        out_shape=jax.ShapeDtypeStruct((n, n_heads, dim), F32),
        compiler_params=_params("arbitrary"),
        name="sample_attn",
    )(q, k_new, v_new, *views)


def _layer_norm(x, g, b):
    xc = x - jnp.mean(x, axis=-1, keepdims=True)
    var = jnp.mean(xc * xc, axis=-1, keepdims=True)
    return xc * lax.rsqrt(var + EPS) * g + b


def _prompt_gate_kernel(chunks, u_ref, vb_ref, g_ref, b_ref, w_ref, bs_ref, o_ref):
    vb = vb_ref[...]
    mean = jnp.sum(jnp.sum(vb, axis=0), axis=-1, keepdims=True) / B_WIDTH
    xc = vb - mean[None]
    var = jnp.sum(jnp.sum(xc * xc, axis=0), axis=-1, keepdims=True) / B_WIDTH
    vn = (xc * lax.rsqrt(var + EPS)[None] * g_ref[...] + b_ref[...]).astype(BF16)
    row = lax.broadcasted_iota(jnp.int32, (CHUNK, CHUNK), 0)
    col = lax.broadcasted_iota(jnp.int32, (CHUNK, CHUNK), 1)
    causal = col <= row
    bs = bs_ref[...]
    for g in range(B_GROUPS):
        cols = slice(g * LANES, (g + 1) * LANES)
        w = jnp.where(causal, w_ref[g], 0.0).astype(BF16)
        for c in range(chunks):
            rows = slice(c * CHUNK, (c + 1) * CHUNK)
            s = _dot(w, vn[g, rows, :]) + bs[:, g:g + 1]
            o_ref[rows, cols] = (u_ref[g, rows, :] * s).astype(o_ref.dtype)


def prompt_gate(z, ln_g, ln_b, w_s, b_s_t, n_rows, chunks):
    rows = chunks * CHUNK
    assert n_rows % rows == 0 and A_HEADS == B_GROUPS
    vec = pl.BlockSpec((B_GROUPS, 1, LANES), lambda c: (0, 0, 0))
    u, vb = z, z
    return pl.pallas_call(
        functools.partial(_prompt_gate_kernel, chunks),
        grid=(n_rows // rows,),
        in_specs=[pl.BlockSpec((B_GROUPS, rows, LANES), lambda c: (3, c, 0)),
                  pl.BlockSpec((B_GROUPS, rows, LANES), lambda c: (4, c, 0)), vec, vec,
                  pl.BlockSpec((B_GROUPS, CHUNK, CHUNK), lambda c: (0, 0, 0)),
                  pl.BlockSpec((CHUNK, B_GROUPS), lambda c: (0, 0))],
        out_specs=pl.BlockSpec((rows, B_WIDTH), lambda c: (c, 0)),
        out_shape=jax.ShapeDtypeStruct((z.shape[1], B_WIDTH), BF16),
        compiler_params=_params("arbitrary"),
        name="prompt_gate",
    )(u, vb, ln_g, ln_b, w_s, b_s_t)


def _sample_gate_kernel(u_ref, vb_ref, g_ref, b_ref, w0_ref, b0_ref, buf_ref, vn_ref, o_ref):
    del buf_ref
    vn = _layer_norm(vb_ref[...], g_ref[...], b_ref[...])
    vn_ref[...] = vn
    o_ref[...] = (u_ref[...] * (w0_ref[...] * vn + b0_ref[...])).astype(o_ref.dtype)


def sample_gate(u, vb, ln_g, ln_b, w0, b0, gate_buf):
    n = u.shape[0]
    assert gate_buf.shape[0] % n == 0
    full = pl.BlockSpec((n, B_WIDTH), lambda i: (0, 0))
    vec = pl.BlockSpec((1, B_WIDTH), lambda i: (0, 0))
    return pl.pallas_call(
        _sample_gate_kernel,
        grid=(1,),
        in_specs=[full, full, vec, vec, vec, vec, pl.BlockSpec(memory_space=pl.ANY)],
        out_specs=[full, pl.BlockSpec((n, B_WIDTH), lambda i: (gate_buf.shape[0] // n - 1, 0))],
        out_shape=[jax.ShapeDtypeStruct((n, B_WIDTH), F32),
                   jax.ShapeDtypeStruct(gate_buf.shape, gate_buf.dtype)],
        input_output_aliases={6: 1},
        compiler_params=_params("arbitrary"),
        name="sample_gate",
    )(u, vb, ln_g, ln_b, w0, b0, gate_buf)


def _prompt_xattn_kernel(q_ref, mk_ref, mv_ref, o_ref):
    scale = HEAD_DIM ** -0.5
    for h in range(MEM_HEADS):
        cols = slice(h * HEAD_DIM, (h + 1) * HEAD_DIM)
        s = _dot_nt(q_ref[:, cols], mk_ref[:, cols]) * scale
        p = jnp.exp(s - jnp.max(s, axis=-1, keepdims=True))
        den = jnp.sum(p, axis=-1, keepdims=True)
        o_ref[:, cols] = (_dot(p, mv_ref[:, cols]) / den).astype(o_ref.dtype)


def prompt_xattn(q, mk, mv, n_batch, seq, tq):
    per = seq // tq
    return pl.pallas_call(
        _prompt_xattn_kernel,
        grid=(n_batch, per),
        in_specs=[pl.BlockSpec((tq, MEM_WIDTH), lambda n, t: (n * per + t, 0)),
                  pl.BlockSpec((MEM_LEN, MEM_WIDTH), lambda n, t: (n, 0)),
                  pl.BlockSpec((MEM_LEN, MEM_WIDTH), lambda n, t: (n, 0))],
        out_specs=pl.BlockSpec((tq, MEM_WIDTH), lambda n, t: (n * per + t, 0)),
        out_shape=jax.ShapeDtypeStruct((q.shape[0], MEM_WIDTH), BF16),
        compiler_params=_params("arbitrary", "arbitrary"),
        name="prompt_xattn",
    )(q, mk, mv)


def _sample_xattn_kernel(per_step, q_ref, mk_ref, mv_ref, o_ref):
    scale = HEAD_DIM ** -0.5
    for b in range(per_step):
        q = q_ref[b] * scale
        s = jnp.sum(mk_ref[b] * q[None], axis=-1, keepdims=True)
        p = jnp.exp(s - jnp.max(s, axis=0)[None])
        o_ref[b] = jnp.sum(p * mv_ref[b], axis=0) / jnp.sum(p, axis=0)


def sample_xattn(q, mk, mv, per_step):
    n, mem_len, n_heads, dim = mk.shape
    q_spec = pl.BlockSpec((per_step, n_heads, dim), lambda i: (i, 0, 0))
    mem_spec = pl.BlockSpec((per_step, mem_len, n_heads, dim), lambda i: (i, 0, 0, 0))
    return pl.pallas_call(
        functools.partial(_sample_xattn_kernel, per_step),
        grid=(n // per_step,),
        in_specs=[q_spec, mem_spec, mem_spec],
        out_specs=q_spec,
        out_shape=jax.ShapeDtypeStruct((n, n_heads, dim), F32),
        compiler_params=_params("arbitrary"),
        name="sample_xattn",
    )(q, mk, mv)


def _rope_tables(seq, n_batch, n_sample, identity_rows):
    half = HEAD_DIM // 2
    inv = jnp.exp(-math.log(ROPE_THETA) * jnp.arange(half, dtype=F32) / half)
    pos = jnp.concatenate([jnp.tile(jnp.arange(seq, dtype=jnp.int32), n_batch),
                           jnp.full((n_sample,), PAST_LEN, jnp.int32)])
    ang = pos.astype(F32)[:, None] * inv[None, :]
    cos, sin = jnp.cos(ang), jnp.sin(ang)
    cos_full = jnp.concatenate([cos, cos], axis=1)
    sin_signed = jnp.concatenate([-sin, sin], axis=1)
    return (jnp.concatenate([cos_full, jnp.ones((identity_rows, HEAD_DIM), F32)], axis=0),
            jnp.concatenate([sin_signed, jnp.zeros((identity_rows, HEAD_DIM), F32)], axis=0))


def kernel(x_prompt, x_sample, mem_prompt, cache_win_k, cache_win_v, cache_mem_k, cache_mem_v, norm_mix, w_in, ln_v_g, ln_v_b, w_spatial, b_spatial, w_out, norm_xattn, w_xq, w_xk, w_xv, w_xo, norm_ffn, w_up, w_down, norm_final):
    n_b, seq, _ = x_prompt.shape
    n_s = x_sample.shape[0]
    n_p = n_b * seq
    depth = w_in.shape[0]
    assert depth == 1 and x_sample.shape[1] == 1
    l = 0
    tm = 1040

    cos, sin_signed = _rope_tables(seq, n_b, n_s, tm)
    row = lambda a: a.reshape(1, -1)

    xp = x_prompt.reshape(n_p, D_MODEL)
    xs = x_sample.reshape(n_s, D_MODEL)
    xn = stack_and_norm(xp, xs, row(norm_mix[l]), 512)
    z = in_proj(xn, w_in[l], cos, sin_signed, tm, 512)
    segment = lambda rows, c: jnp.transpose(rows[c * A_HEADS:(c + 1) * A_HEADS], (1, 0, 2))
    z_p, z_s = z[:, :n_p], z[:, n_p:]
    q_s, k_s, v_s, u_s, vb_s = (segment(z_s, c) for c in range(5))

    assert seq == PAST_LEN and cache_win_k.shape[2] == PAST_LEN and seq == STEPS * DILATIONS[-1]
    by_head = lambda a: a.reshape(n_s, -1, HEAD_DIM)
    a_s = sample_attention(q_s, k_s, v_s, cache_win_k[l], cache_win_v[l], 2)
    a = write_tail(prompt_attention(z, n_b, seq, 2), a_s.reshape(n_s, A_WIDTH))

    slabs = lambda a: a.reshape(B_GROUPS, 1, LANES)
    g_p = prompt_gate(z, slabs(ln_v_g[l]), slabs(ln_v_b[l]), w_spatial[l], b_spatial[l].T, n_p, 2)
    w0 = jnp.repeat(w_spatial[l][:, 0, 0], LANES).reshape(1, B_WIDTH)
    b0 = jnp.repeat(b_spatial[l][:, 0], LANES).reshape(1, B_WIDTH)
    vn_s, g = sample_gate(u_s.reshape(n_s, B_WIDTH), vb_s.reshape(n_s, B_WIDTH),
                          row(ln_v_g[l]), row(ln_v_b[l]), w0, b0, g_p)

    x1 = out_proj(a, g, w_out[l], xp, xs, tm, 512)

    qx = norm_proj(x1, row(norm_xattn[l]), w_xq[l], 520)
    mk, mv = matmul_pair(mem_prompt.reshape(n_b * MEM_LEN, D_MODEL), w_xk[l], w_xv[l], 512)
    ox_s = sample_xattn(by_head(qx[n_p:]), cache_mem_k[l], cache_mem_v[l], 4)
    ox = write_tail(prompt_xattn(qx, mk, mv, n_b, seq, 512), ox_s.reshape(n_s, MEM_WIDTH))
    x2, h2 = proj_res_norm(ox, w_xo[l], x1, row(norm_ffn[l]), 416)

    act = matmul(h2, w_up[l], tm, 1024, BF16, body=_mm_relu2_kernel, name="ffn_up", x_buffers=1)
    x3 = matmul_ksplit_residual(act, w_down[l], x2, 2 * tm, 1024, 1024, tm)
    y_p, y_s = final_norm(x3, row(norm_final), n_p, n_s, 512)

    return (y_p.reshape(n_b, seq, D_MODEL),
            y_s.reshape(n_s, 1, D_MODEL),
            segment(z_p, 1).reshape(depth, n_b, seq, A_HEADS, HEAD_DIM),
            segment(z_p, 2).reshape(depth, n_b, seq, A_HEADS, HEAD_DIM),
            mk.reshape(depth, n_b, MEM_LEN, MEM_HEADS, HEAD_DIM),
            mv.reshape(depth, n_b, MEM_LEN, MEM_HEADS, HEAD_DIM),
            k_s.reshape(depth, n_s, 1, A_HEADS, HEAD_DIM),
            v_s.reshape(depth, n_s, 1, A_HEADS, HEAD_DIM),
            vn_s.reshape(depth, n_s, 1, B_WIDTH))
```
